```python
import math
import jax, jax.numpy as jnp
from jax import lax
import numpy as np

D_MODEL = 1024
BATCH = 8
SEQ = 8192
DEPTH = 1
DEC_BATCH = 8
DEC_SEQ = 2048
PAST_LEN = 128

D_SSM = D_MODEL // 2
SSM_GROUP = 16
N_SSM_GROUPS = D_SSM // SSM_GROUP
SSM_STATE = 64
D_HYENA = D_MODEL - D_SSM
HYENA_ORDER = 2
HYENA_SHORT = 3
HYENA_BANDS = 8
HYENA_POS_DIM = 1 + 2 * HYENA_BANDS
HYENA_FILTER_HIDDEN = 64
HYENA_TIME_SCALE = 4096.0
HYENA_MAX_PERIOD = 10000.0
N_FILTERS = HYENA_ORDER * 2 * D_HYENA
D_IN = D_SSM + (HYENA_ORDER + 1) * D_HYENA
D_FF = 128 * math.ceil(8 * D_MODEL / 3 / 128)
LN_EPS = 1e-5
RMS_EPS = 1e-6
FILTER_EPS = 1e-6
DEEPNORM_ALPHA = (2.0 * DEPTH) ** 0.25
DEEPNORM_BETA = (8.0 * DEPTH) ** -0.25

kernel_name = "hybrid_s5_hyena_macaron_encoder"

F32 = jnp.float32


def _layer_norm(x, g, b):
    xf = x.astype(F32)
    mu = jnp.mean(xf, axis=-1, keepdims=True)
    xc = xf - mu
    var = jnp.mean(xc * xc, axis=-1, keepdims=True)
    return (xc * lax.rsqrt(var + LN_EPS) * g.astype(F32) + b.astype(F32)).astype(x.dtype)


def _rms_norm(x, g, dtype):
    xf = x.astype(F32)
    ms = jnp.mean(xf * xf, axis=-1, keepdims=True)
    return (xf * lax.rsqrt(ms + RMS_EPS) * g.astype(F32)).astype(dtype)


def _swiglu(x, w_gate, w_up, w_down):
    return (jax.nn.silu(x @ w_gate) * (x @ w_up)) @ w_down


def _ffn_sublayer(x, w_gate, w_up, w_down, ln_g, ln_b):
    return _layer_norm(DEEPNORM_ALPHA * x + 0.5 * _swiglu(x, w_gate, w_up, w_down), ln_g, ln_b)


def _short_conv(x, w, b):
    L = x.shape[1]
    pad = HYENA_SHORT // 2
    xp = jnp.pad(x, ((0, 0), (pad, pad), (0, 0)))
    y = b
    for j in range(HYENA_SHORT):
        y = y + xp[:, j:j + L] * w[j]
    return y


def _complex_linear_combine(e1, e2):
    a1r, a1i, b1r, b1i = e1
    a2r, a2i, b2r, b2i = e2
    ar = a2r * a1r - a2i * a1i
    ai = a2r * a1i + a2i * a1r
    br = a2r * b1r - a2i * b1i + b2r
    bi = a2r * b1i + a2i * b1r + b2i
    return ar, ai, br, bi


def _s5_discretize(lam_re, lam_im, log_step, b_re, b_im):
    step = jnp.exp(log_step)[:, None]
    mag = jnp.exp(lam_re * step)
    ar = mag * jnp.cos(lam_im * step)
    ai = mag * jnp.sin(lam_im * step)
    nr = ar - 1.0
    ni = ai
    den = lam_re * lam_re + lam_im * lam_im
    qr = (nr * lam_re + ni * lam_im) / den
    qi = (ni * lam_re - nr * lam_im) / den
    bbr = qr[..., None] * b_re - qi[..., None] * b_im
    bbi = qr[..., None] * b_im + qi[..., None] * b_re
    return ar, ai, bbr, bbi


def _s5_scan(u, ar, ai, bbr, bbi, c_re, c_im, reverse):
    L = u.shape[0]
    bur = jnp.einsum('lgh,gph->lgp', u, bbr)
    bui = jnp.einsum('lgh,gph->lgp', u, bbi)
    a_r = jnp.broadcast_to(ar, (L,) + ar.shape)
    a_i = jnp.broadcast_to(ai, (L,) + ai.shape)
    _, _, hr, hi = lax.associative_scan(_complex_linear_combine, (a_r, a_i, bur, bui),
                                        reverse=reverse, axis=0)
    return jnp.einsum('lgp,ghp->lgh', hr, c_re) - jnp.einsum('lgp,ghp->lgh', hi, c_im)


def _s5_mixer(u, lam_re, lam_im, log_step, b_re, b_im, c_re, c_im, d):
    nb, L, _ = u.shape
    ug = u.astype(F32).reshape(nb, L, N_SSM_GROUPS, SSM_GROUP)
    lam_re, lam_im, log_step = lam_re.astype(F32), lam_im.astype(F32), log_step.astype(F32)
    b_re, b_im, c_re, c_im = b_re.astype(F32), b_im.astype(F32), c_re.astype(F32), c_im.astype(F32)
    d = d.astype(F32)
    fwd = _s5_discretize(lam_re[0], lam_im[0], log_step[0], b_re[0], b_im[0])
    bwd = _s5_discretize(lam_re[1], lam_im[1], log_step[1], b_re[1], b_im[1])

    def one_sequence(us):
        yf = _s5_scan(us, *fwd, c_re[0], c_im[0], False)
        yb = _s5_scan(us, *bwd, c_re[1], c_im[1], True)
        return yf + yb + d * us

    y = lax.map(one_sequence, ug)
    return y.reshape(nb, L, D_SSM)


def _hyena_filter_spectrum(L, w1, b1, w2, b2, w3, sin_freq, log_decay):
    w1, b1, w2, b2, w3 = (a.astype(F32) for a in (w1, b1, w2, b2, w3))
    sin_freq, log_decay = sin_freq.astype(F32), log_decay.astype(F32)
    t = jnp.arange(L, dtype=F32)
    t_lin = t / HYENA_TIME_SCALE
    omega = jnp.exp(-math.log(HYENA_MAX_PERIOD) * jnp.arange(HYENA_BANDS, dtype=F32) / HYENA_BANDS)
    ang = t[:, None] * omega[None, :]
    feats = jnp.concatenate([t_lin[:, None], jnp.sin(ang), jnp.cos(ang)], axis=-1)
    hdn = jnp.sin(sin_freq[0] * (feats @ w1 + b1))
    hdn = jnp.sin(sin_freq[1] * (hdn @ w2 + b2))
    filt = (hdn @ w3) * jnp.exp(-t_lin[:, None] * jnp.exp(log_decay)[None, :])
    filt = filt.reshape(L, HYENA_ORDER, 2, D_HYENA)
    fwd = filt[:, :, 0]
    bwd = filt[:, :, 1]
    k = jnp.concatenate([fwd, jnp.zeros((1, HYENA_ORDER, D_HYENA), F32), bwd[:0:-1]], axis=0)
    k = k / (jnp.sum(jnp.abs(k), axis=0, keepdims=True) + FILTER_EPS)
    return jnp.fft.rfft(k, axis=0)


def _hyena_mixer(v, gates, k_f, bias):
    L = v.shape[1]
    n = 2 * L
    z = v
    for o in range(HYENA_ORDER):
        zf = jnp.fft.rfft(z, n=n, axis=1)
        y = jnp.fft.irfft(zf * k_f[None, :, o], n=n, axis=1)[:, :L]
        z = gates[o] * (y + bias[o] * z)
    return z


def _mixing_sublayer(h, w_in,
                     ssm_lam_re, ssm_lam_im, ssm_log_step, ssm_b_re, ssm_b_im, ssm_c_re, ssm_c_im,
                     ssm_d, ssm_glu_w, ssm_glu_b, ssm_norm_g,
                     hy_short_w, hy_short_b, hy_filt_w1, hy_filt_b1, hy_filt_w2, hy_filt_b2,
                     hy_filt_w3, hy_sin_freq, hy_log_decay, hy_bias, hy_norm_g, w_out):
    proj = h @ w_in
    u = proj[..., :D_SSM]
    hy = _short_conv(proj[..., D_SSM:], hy_short_w, hy_short_b).astype(F32)
    v = hy[..., :D_HYENA]
    gates = [hy[..., D_HYENA * (o + 1):D_HYENA * (o + 2)] for o in range(HYENA_ORDER)]

    y_ssm = _s5_mixer(u, ssm_lam_re, ssm_lam_im, ssm_log_step, ssm_b_re, ssm_b_im,
                      ssm_c_re, ssm_c_im, ssm_d)
    g = jax.nn.gelu(y_ssm)
    y_ssm = g * jax.nn.sigmoid(g @ ssm_glu_w.astype(F32) + ssm_glu_b.astype(F32))

    k_f = _hyena_filter_spectrum(h.shape[1], hy_filt_w1, hy_filt_b1, hy_filt_w2, hy_filt_b2,
                                 hy_filt_w3, hy_sin_freq, hy_log_decay)
    y_hy = _hyena_mixer(v, gates, k_f, hy_bias.astype(F32))

    mixed = jnp.concatenate([_rms_norm(y_ssm, ssm_norm_g, h.dtype),
                             _rms_norm(y_hy, hy_norm_g, h.dtype)], axis=-1)
    return mixed @ w_out


def _run_trunk(x, ffn1_w_gate, ffn1_w_up, ffn1_w_down, ln1_g, ln1_b, w_in,
               ssm_lam_re, ssm_lam_im, ssm_log_step, ssm_b_re, ssm_b_im, ssm_c_re, ssm_c_im,
               ssm_d, ssm_glu_w, ssm_glu_b, ssm_norm_g,
               hy_short_w, hy_short_b, hy_filt_w1, hy_filt_b1, hy_filt_w2, hy_filt_b2, hy_filt_w3,
               hy_sin_freq, hy_log_decay, hy_bias, hy_norm_g, w_out, ln2_g, ln2_b,
               ffn2_w_gate, ffn2_w_up, ffn2_w_down, ln3_g, ln3_b):
    for l in range(DEPTH):
        x = _ffn_sublayer(x, ffn1_w_gate[l], ffn1_w_up[l], ffn1_w_down[l], ln1_g[l], ln1_b[l])
        mix = _mixing_sublayer(x, w_in[l],
                               ssm_lam_re[l], ssm_lam_im[l], ssm_log_step[l], ssm_b_re[l], ssm_b_im[l],
                               ssm_c_re[l], ssm_c_im[l], ssm_d[l], ssm_glu_w[l], ssm_glu_b[l], ssm_norm_g[l],
                               hy_short_w[l], hy_short_b[l], hy_filt_w1[l], hy_filt_b1[l], hy_filt_w2[l],
                               hy_filt_b2[l], hy_filt_w3[l], hy_sin_freq[l], hy_log_decay[l], hy_bias[l],
                               hy_norm_g[l], w_out[l])
        x = _layer_norm(DEEPNORM_ALPHA * x + mix, ln2_g[l], ln2_b[l])
        x = _ffn_sublayer(x, ffn2_w_gate[l], ffn2_w_up[l], ffn2_w_down[l], ln3_g[l], ln3_b[l])
    return x


def setup_inputs(seed: int = 0) -> dict:
    key = jax.random.key(seed)
    ks = iter(jax.random.split(key, 48))
    nrm = lambda shape, scale: scale * jax.random.normal(next(ks), shape, F32)
    G, H, P = N_SSM_GROUPS, SSM_GROUP, SSM_STATE
    Dp = DEPTH

    x_prompt = jax.random.normal(next(ks), (BATCH, SEQ, D_MODEL), F32)
    x_sample = jax.random.normal(next(ks), (DEC_BATCH, DEC_SEQ, D_MODEL), F32)

    ffn1_w_gate = nrm((Dp, D_MODEL, D_FF), D_MODEL ** -0.5)
    ffn1_w_up = nrm((Dp, D_MODEL, D_FF), D_MODEL ** -0.5)
    ffn1_w_down = nrm((Dp, D_FF, D_MODEL), DEEPNORM_BETA * D_FF ** -0.5)
    ln1_g = 1.0 + nrm((Dp, D_MODEL), 0.02)
    ln1_b = nrm((Dp, D_MODEL), 0.02)

    w_in = nrm((Dp, D_MODEL, D_IN), D_MODEL ** -0.5)

    ssm_lam_re = -0.5 + nrm((Dp, 2, G, P), 0.01)
    ssm_lam_im = jnp.pi * jnp.arange(P, dtype=F32) + nrm((Dp, 2, G, P), 0.01)
    ssm_log_step = jax.random.uniform(next(ks), (Dp, 2, G), F32, math.log(1e-3), math.log(1e-1))
    ssm_b_re = nrm((Dp, 2, G, P, H), (2.0 * H) ** -0.5)
    ssm_b_im = nrm((Dp, 2, G, P, H), (2.0 * H) ** -0.5)
    ssm_c_re = nrm((Dp, 2, G, H, P), (2.0 * P) ** -0.5)
    ssm_c_im = nrm((Dp, 2, G, H, P), (2.0 * P) ** -0.5)
    ssm_d = nrm((Dp, G, H), 1.0)
    ssm_glu_w = nrm((Dp, D_SSM, D_SSM), D_SSM ** -0.5)
    ssm_glu_b = nrm((Dp, D_SSM), 0.02)
    ssm_norm_g = 1.0 + nrm((Dp, D_SSM), 0.02)

    hy_short_w = nrm((Dp, HYENA_SHORT, (HYENA_ORDER + 1) * D_HYENA), HYENA_SHORT ** -0.5)
    hy_short_b = nrm((Dp, (HYENA_ORDER + 1) * D_HYENA), 0.02)
    hy_filt_w1 = nrm((Dp, HYENA_POS_DIM, HYENA_FILTER_HIDDEN), HYENA_POS_DIM ** -0.5)
    hy_filt_b1 = nrm((Dp, HYENA_FILTER_HIDDEN), 0.1)
    hy_filt_w2 = nrm((Dp, HYENA_FILTER_HIDDEN, HYENA_FILTER_HIDDEN), HYENA_FILTER_HIDDEN ** -0.5)
    hy_filt_b2 = nrm((Dp, HYENA_FILTER_HIDDEN), 0.1)
    hy_filt_w3 = nrm((Dp, HYENA_FILTER_HIDDEN, N_FILTERS), HYENA_FILTER_HIDDEN ** -0.5)
    hy_sin_freq = 1.0 + nrm((Dp, 2, HYENA_FILTER_HIDDEN), 0.1)
    fast, slow = math.log(abs(math.log(1e-2)) / 0.3), math.log(abs(math.log(1e-2)) / 1.5)
    base_decay = jnp.tile(jnp.linspace(fast, slow, D_HYENA, dtype=F32), HYENA_ORDER * 2)
    hy_log_decay = base_decay + nrm((Dp, N_FILTERS), 0.01)
    hy_bias = nrm((Dp, HYENA_ORDER, D_HYENA), 1.0)
    hy_norm_g = 1.0 + nrm((Dp, D_HYENA), 0.02)

    w_out = nrm((Dp, D_MODEL, D_MODEL), DEEPNORM_BETA * D_MODEL ** -0.5)
    ln2_g = 1.0 + nrm((Dp, D_MODEL), 0.02)
    ln2_b = nrm((Dp, D_MODEL), 0.02)

    ffn2_w_gate = nrm((Dp, D_MODEL, D_FF), D_MODEL ** -0.5)
    ffn2_w_up = nrm((Dp, D_MODEL, D_FF), D_MODEL ** -0.5)
    ffn2_w_down = nrm((Dp, D_FF, D_MODEL), DEEPNORM_BETA * D_FF ** -0.5)
    ln3_g = 1.0 + nrm((Dp, D_MODEL), 0.02)
    ln3_b = nrm((Dp, D_MODEL), 0.02)

    return {
        "x_prompt": x_prompt, "x_sample": x_sample,
        "ffn1_w_gate": ffn1_w_gate, "ffn1_w_up": ffn1_w_up, "ffn1_w_down": ffn1_w_down,
        "ln1_g": ln1_g, "ln1_b": ln1_b, "w_in": w_in,
        "ssm_lam_re": ssm_lam_re, "ssm_lam_im": ssm_lam_im, "ssm_log_step": ssm_log_step,
        "ssm_b_re": ssm_b_re, "ssm_b_im": ssm_b_im, "ssm_c_re": ssm_c_re, "ssm_c_im": ssm_c_im,
        "ssm_d": ssm_d, "ssm_glu_w": ssm_glu_w, "ssm_glu_b": ssm_glu_b, "ssm_norm_g": ssm_norm_g,
        "hy_short_w": hy_short_w, "hy_short_b": hy_short_b,
        "hy_filt_w1": hy_filt_w1, "hy_filt_b1": hy_filt_b1, "hy_filt_w2": hy_filt_w2,
        "hy_filt_b2": hy_filt_b2, "hy_filt_w3": hy_filt_w3, "hy_sin_freq": hy_sin_freq,
        "hy_log_decay": hy_log_decay, "hy_bias": hy_bias, "hy_norm_g": hy_norm_g,
        "w_out": w_out, "ln2_g": ln2_g, "ln2_b": ln2_b,
        "ffn2_w_gate": ffn2_w_gate, "ffn2_w_up": ffn2_w_up, "ffn2_w_down": ffn2_w_down,
        "ln3_g": ln3_g, "ln3_b": ln3_b,
    }


def reference(x_prompt, x_sample, ffn1_w_gate, ffn1_w_up, ffn1_w_down, ln1_g, ln1_b, w_in,
              ssm_lam_re, ssm_lam_im, ssm_log_step, ssm_b_re, ssm_b_im, ssm_c_re, ssm_c_im,
              ssm_d, ssm_glu_w, ssm_glu_b, ssm_norm_g,
              hy_short_w, hy_short_b, hy_filt_w1, hy_filt_b1, hy_filt_w2, hy_filt_b2, hy_filt_w3,
              hy_sin_freq, hy_log_decay, hy_bias, hy_norm_g, w_out, ln2_g, ln2_b,
              ffn2_w_gate, ffn2_w_up, ffn2_w_down, ln3_g, ln3_b):
    y_prompt = _run_trunk(x_prompt, ffn1_w_gate, ffn1_w_up, ffn1_w_down, ln1_g, ln1_b, w_in,
                          ssm_lam_re, ssm_lam_im, ssm_log_step, ssm_b_re, ssm_b_im, ssm_c_re, ssm_c_im,
                          ssm_d, ssm_glu_w, ssm_glu_b, ssm_norm_g,
                          hy_short_w, hy_short_b, hy_filt_w1, hy_filt_b1, hy_filt_w2, hy_filt_b2, hy_filt_w3,
                          hy_sin_freq, hy_log_decay, hy_bias, hy_norm_g, w_out, ln2_g, ln2_b,
                          ffn2_w_gate, ffn2_w_up, ffn2_w_down, ln3_g, ln3_b)
    y_sample = _run_trunk(x_sample, ffn1_w_gate, ffn1_w_up, ffn1_w_down, ln1_g, ln1_b, w_in,
                          ssm_lam_re, ssm_lam_im, ssm_log_step, ssm_b_re, ssm_b_im, ssm_c_re, ssm_c_im,
                          ssm_d, ssm_glu_w, ssm_glu_b, ssm_norm_g,
                          hy_short_w, hy_short_b, hy_filt_w1, hy_filt_b1, hy_filt_w2, hy_filt_b2, hy_filt_w3,
                          hy_sin_freq, hy_log_decay, hy_bias, hy_norm_g, w_out, ln2_g, ln2_b,
                          ffn2_w_gate, ffn2_w_up, ffn2_w_down, ln3_g, ln3_b)
    return (y_prompt, y_sample)
```

```python
import functools
import math

import jax
import jax.numpy as jnp
from jax import lax
from jax.experimental import pallas as pl
from jax.experimental.pallas import tpu as pltpu

F32 = jnp.float32
BF16 = jnp.bfloat16

D_MODEL = 1024
D_SSM = 512
SSM_GROUP = 16
N_SSM_GROUPS = D_SSM // SSM_GROUP
SSM_STATE = 64
D_HYENA = 512
HYENA_ORDER = 2
HYENA_BANDS = 8
HYENA_POS_DIM = 1 + 2 * HYENA_BANDS
HYENA_FILTER_HIDDEN = 64
HYENA_TIME_SCALE = 4096.0
HYENA_MAX_PERIOD = 10000.0
D_HY_IN = (HYENA_ORDER + 1) * D_HYENA
D_IN = D_SSM + D_HY_IN
D_FF = 128 * math.ceil(8 * D_MODEL / 3 / 128)
LN_EPS = 1e-5
RMS_EPS = 1e-6
FILTER_EPS = 1e-6
DEPTH = 1
DEEPNORM_ALPHA = (2.0 * DEPTH) ** 0.25

LANE = 128
SUBLANE = 8
V7X_VMEM_BYTES = 64 * 1024 * 1024
VMEM_LIMIT_BYTES = V7X_VMEM_BYTES - 8 * 1024 * 1024

TOKEN_TILE = 512
FF_CHUNK = 256
CONV_TILE = 512
S5_CHUNK = 16
S5_LANES = S5_CHUNK * LANE
S5_STATE_LANES = (LANE // SSM_GROUP) * SSM_STATE
K1_BLOCK = 8


def _params(semantics):
    return pltpu.CompilerParams(dimension_semantics=semantics, vmem_limit_bytes=VMEM_LIMIT_BYTES)


def _resident(shape):
    nd = len(shape)
    return pl.BlockSpec(shape, lambda *_: (0,) * nd, pipeline_mode=pl.Buffered(1))


def _layer_norm(x, g, b):
    mu = jnp.mean(x, axis=-1, keepdims=True)
    xc = x - mu
    var = jnp.mean(xc * xc, axis=-1, keepdims=True)
    return xc * lax.rsqrt(var + LN_EPS) * g + b


def _rms_norm(x, g):
    ms = jnp.mean(x * x, axis=-1, keepdims=True)
    return x * lax.rsqrt(ms + RMS_EPS) * g


def _ffn_ln(x, wg_ref, wu_ref, wd_ref, g_ref, b_ref, acc_ref):
    xb = x.astype(BF16)
    acc_ref[...] = jnp.zeros_like(acc_ref)

    def body(c, carry):
        off = pl.multiple_of(c * FF_CHUNK, FF_CHUNK)
        gate = jnp.dot(xb, wg_ref[:, pl.ds(off, FF_CHUNK)], preferred_element_type=F32)
        up = jnp.dot(xb, wu_ref[:, pl.ds(off, FF_CHUNK)], preferred_element_type=F32)
        act = (jax.nn.silu(gate) * up).astype(BF16)
        acc_ref[...] += jnp.dot(act, wd_ref[pl.ds(off, FF_CHUNK), :], preferred_element_type=F32)
        return carry

    lax.fori_loop(0, D_FF // FF_CHUNK, body, 0)
    return _layer_norm(DEEPNORM_ALPHA * x + 0.5 * acc_ref[...], g_ref[...], b_ref[...])


def _ffn1_kernel(x_ref, wg_ref, wu_ref, wd_ref, g_ref, b_ref, win_ref, h_ref, u_ref, hy_ref, acc_ref):
    h = _ffn_ln(x_ref[...], wg_ref, wu_ref, wd_ref, g_ref, b_ref, acc_ref)
    h_ref[...] = h
    hb = h.astype(BF16)
    u_ref[...] = jnp.dot(hb, win_ref[:, :D_SSM], preferred_element_type=F32)
    hy_ref[...] = jnp.dot(hb, win_ref[:, D_SSM:], preferred_element_type=F32)


def _ffn1_call(x2d, p):
    t = x2d.shape[0]
    tm = TOKEN_TILE
    row = lambda w: pl.BlockSpec((tm, w), lambda i: (i, 0))
    return pl.pallas_call(
        _ffn1_kernel,
        out_shape=(jax.ShapeDtypeStruct((t, D_MODEL), F32),
                   jax.ShapeDtypeStruct((t, D_SSM), F32),
                   jax.ShapeDtypeStruct((t, D_HY_IN), F32)),
        grid=(t // tm,),
        in_specs=[row(D_MODEL), _resident((D_MODEL, D_FF)), _resident((D_MODEL, D_FF)),
                  _resident((D_FF, D_MODEL)), _resident((1, D_MODEL)), _resident((1, D_MODEL)),
                  _resident((D_MODEL, D_IN))],
        out_specs=(row(D_MODEL), row(D_SSM), row(D_HY_IN)),
        scratch_shapes=[pltpu.VMEM((tm, D_MODEL), F32)],
        compiler_params=_params(("parallel",)),
        name="ffn1_ln1_proj",
    )(x2d, p["ffn1_wg"], p["ffn1_wu"], p["ffn1_wd"], p["ln1_g"], p["ln1_b"], p["w_in"])


def _final_kernel(ys_ref, yh_ref, h_ref, gw_ref, gb_ref, sg_ref, hg_ref, wo_ref, l2g_ref, l2b_ref,
                  wg_ref, wu_ref, wd_ref, l3g_ref, l3b_ref, o_ref, acc_ref):
    g = jax.nn.gelu(ys_ref[...])
    gate = jax.nn.sigmoid(jnp.dot(g.astype(BF16), gw_ref[...], preferred_element_type=F32) + gb_ref[...])
    y_ssm = _rms_norm(g * gate, sg_ref[...])
    y_hy = _rms_norm(yh_ref[...], hg_ref[...])
    mix = (jnp.dot(y_ssm.astype(BF16), wo_ref[:D_SSM, :], preferred_element_type=F32)
           + jnp.dot(y_hy.astype(BF16), wo_ref[D_SSM:, :], preferred_element_type=F32))
    x2 = _layer_norm(DEEPNORM_ALPHA * h_ref[...] + mix, l2g_ref[...], l2b_ref[...])
    o_ref[...] = _ffn_ln(x2, wg_ref, wu_ref, wd_ref, l3g_ref, l3b_ref, acc_ref)


def _final_call(y_ssm, y_hy, h1, p):
    t = h1.shape[0]
    tm = TOKEN_TILE
    row = lambda w: pl.BlockSpec((tm, w), lambda i: (i, 0))
    return pl.pallas_call(
        _final_kernel,
        out_shape=jax.ShapeDtypeStruct((t, D_MODEL), F32),
        grid=(t // tm,),
        in_specs=[row(D_SSM), row(D_HYENA), row(D_MODEL),
                  _resident((D_SSM, D_SSM)), _resident((1, D_SSM)), _resident((1, D_SSM)),
                  _resident((1, D_HYENA)), _resident((D_MODEL, D_MODEL)),
                  _resident((1, D_MODEL)), _resident((1, D_MODEL)),
                  _resident((D_MODEL, D_FF)), _resident((D_MODEL, D_FF)), _resident((D_FF, D_MODEL)),
                  _resident((1, D_MODEL)), _resident((1, D_MODEL))],
        out_specs=row(D_MODEL),
        scratch_shapes=[pltpu.VMEM((tm, D_MODEL), F32)],
        compiler_params=_params(("parallel",)),
        name="mix_ln2_ffn2_ln3",
    )(y_ssm, y_hy, h1, p["glu_w"], p["glu_b"], p["ssm_norm_g"], p["hy_norm_g"], p["w_out"],
      p["ln2_g"], p["ln2_b"], p["ffn2_wg"], p["ffn2_wu"], p["ffn2_wd"], p["ln3_g"], p["ln3_b"])


def _shortconv_kernel(x_ref, prev_ref, next_ref, w_ref, b_ref, o_ref):
    j = pl.program_id(1)
    x = x_ref[...]
    rows = x.shape[0]
    row = lax.broadcasted_iota(jnp.int32, x.shape, 0)
    before = jnp.where(j > 0, prev_ref[SUBLANE - 1:SUBLANE, :], 0.0)
    after = jnp.where(j < pl.num_programs(1) - 1, next_ref[0:1, :], 0.0)
    x_m1 = jnp.where(row == 0, before, pltpu.roll(x, 1, 0))
    x_p1 = jnp.where(row == rows - 1, after, pltpu.roll(x, rows - 1, 0))
    o_ref[...] = ((b_ref[...] + x_m1 * w_ref[0:1, :]) + x * w_ref[1:2, :]) + x_p1 * w_ref[2:3, :]


def _shortconv_call(hy_raw, p):
    b, l, c = hy_raw.shape
    tt = CONV_TILE
    per = tt // SUBLANE
    last = l // SUBLANE - 1
    return pl.pallas_call(
        _shortconv_kernel,
        out_shape=jax.ShapeDtypeStruct((b, l, c), F32),
        grid=(b, l // tt),
        in_specs=[pl.BlockSpec((None, tt, c), lambda i, j: (i, j, 0)),
                  pl.BlockSpec((None, SUBLANE, c), lambda i, j: (i, jnp.maximum(j * per - 1, 0), 0)),
                  pl.BlockSpec((None, SUBLANE, c), lambda i, j: (i, jnp.minimum((j + 1) * per, last), 0)),
                  _resident((3, c)), _resident((1, c))],
        out_specs=pl.BlockSpec((None, tt, c), lambda i, j: (i, j, 0)),
        compiler_params=_params(("parallel", "parallel")),
        name="hyena_shortconv",
    )(hy_raw, hy_raw, hy_raw, p["short_w"], p["short_b"])


def _s5_kernel(u_ref, m_ref, ein_ref, gout_ref, lam_ref, d_ref, y_ref, ucat_ref, st_ref, *, nch, rb):
    sl = S5_STATE_LANES
    for t in range(S5_CHUNK):
        ucat_ref[:, t * LANE:(t + 1) * LANE] = u_ref[pl.ds(t, nch, stride=S5_CHUNK), :].astype(BF16)

    def state_in(i, carry):
        r0 = pl.multiple_of(i * rb, rb)
        lhs = ucat_ref[pl.ds(r0, rb), :]
        for cb in range(4):
            st_ref[pl.ds(r0, rb), cb * sl:(cb + 1) * sl] = jnp.dot(
                lhs, ein_ref[:, cb * sl:(cb + 1) * sl], preferred_element_type=F32)
        return carry

    lax.fori_loop(0, nch // rb, state_in, 0)

    lfr, lfi = lam_ref[0:1, :], lam_ref[1:2, :]
    lbr, lbi = lam_ref[2:3, :], lam_ref[3:4, :]

    def scan(i, carry):
        sfr, sfi, sbr, sbi = carry
        j = nch - 1 - i
        xfr = st_ref[pl.ds(i, 1), 0:sl]
        xfi = st_ref[pl.ds(i, 1), sl:2 * sl]
        st_ref[pl.ds(i, 1), 0:sl] = sfr
        st_ref[pl.ds(i, 1), sl:2 * sl] = sfi
        xbr = st_ref[pl.ds(j, 1), 2 * sl:3 * sl]
        xbi = st_ref[pl.ds(j, 1), 3 * sl:4 * sl]
        st_ref[pl.ds(j, 1), 2 * sl:3 * sl] = sbr
        st_ref[pl.ds(j, 1), 3 * sl:4 * sl] = sbi
        return (lfr * sfr - lfi * sfi + xfr, lfr * sfi + lfi * sfr + xfi,
                lbr * sbr - lbi * sbi + xbr, lbr * sbi + lbi * sbr + xbi)

    zero = jnp.zeros((1, sl), F32)
    lax.fori_loop(0, nch, scan, (zero, zero, zero, zero))

    def emit(i, carry):
        r0 = pl.multiple_of(i * rb, rb)
        lhs_u = ucat_ref[pl.ds(r0, rb), :]
        lhs_s = st_ref[pl.ds(r0, rb), :].astype(BF16)
        for cb in range(4):
            cols = slice(cb * 4 * LANE, (cb + 1) * 4 * LANE)
            yc = (jnp.dot(lhs_u, m_ref[:, cols], preferred_element_type=F32)
                  + jnp.dot(lhs_s, gout_ref[:, cols], preferred_element_type=F32))
            for tt in range(4):
                rows = pl.ds(r0 * S5_CHUNK + cb * 4 + tt, rb, stride=S5_CHUNK)
                y_ref[rows, :] = yc[:, tt * LANE:(tt + 1) * LANE] + d_ref[...] * u_ref[rows, :]
        return carry

    lax.fori_loop(0, nch // rb, emit, 0)


def _s5_call(u, p):
    b, l, _ = u.shape
    nch = l // S5_CHUNK
    rb = min(nch, 256)
    nq = D_SSM // LANE
    mat = pl.BlockSpec((None, S5_LANES, S5_LANES), lambda q, i: (q, 0, 0), pipeline_mode=pl.Buffered(1))
    return pl.pallas_call(
        functools.partial(_s5_kernel, nch=nch, rb=rb),
        out_shape=jax.ShapeDtypeStruct((b, l, D_SSM), F32),
        grid=(nq, b),
        in_specs=[pl.BlockSpec((None, l, LANE), lambda q, i: (i, 0, q)),
                  mat, mat, mat,
                  pl.BlockSpec((None, 4, S5_STATE_LANES), lambda q, i: (q, 0, 0)),
                  pl.BlockSpec((None, 1, LANE), lambda q, i: (q, 0, 0))],
        out_specs=pl.BlockSpec((None, l, LANE), lambda q, i: (i, 0, q)),
        scratch_shapes=[pltpu.VMEM((nch, S5_LANES), BF16), pltpu.VMEM((nch, 4 * S5_STATE_LANES), F32)],
        compiler_params=_params(("parallel", "parallel")),
        name="s5_chunked",
    )(u, p["s5_m"], p["s5_ein"], p["s5_gout"], p["s5_lam"], p["s5_d"])


def _s5_tables(lam_re, lam_im, log_step, b_re, b_im, c_re, c_im, d):
    g, hh, pp, tc = N_SSM_GROUPS, SSM_GROUP, SSM_STATE, S5_CHUNK
    gl = LANE // hh
    nq = g // gl
    step = jnp.exp(log_step)[..., None]
    mag = jnp.exp(lam_re * step)
    ar = mag * jnp.cos(lam_im * step)
    ai = mag * jnp.sin(lam_im * step)
    nr, ni = ar - 1.0, ai
    den = lam_re * lam_re + lam_im * lam_im
    qr = (nr * lam_re + ni * lam_im) / den
    qi = (ni * lam_re - nr * lam_im) / den
    bb = lax.complex(qr[..., None] * b_re - qi[..., None] * b_im, qr[..., None] * b_im + qi[..., None] * b_re)
    cc = lax.complex(c_re, c_im)
    lam = lax.complex(ar, ai)
    pw = jnp.concatenate([jnp.ones((1,) + lam.shape, lam.dtype),
                          jnp.cumprod(jnp.broadcast_to(lam, (tc,) + lam.shape), axis=0)], axis=0)
    eye = jnp.eye(gl, dtype=F32)

    kf = jnp.real(jnp.einsum("gap,jgp,gpb->jgab", cc[0], pw[:tc, 0], bb[0]))
    kb = jnp.real(jnp.einsum("gap,jgp,gpb->jgab", cc[1], pw[:tc, 1], bb[1]))
    lag0 = kf[:1] + kb[:1]
    kall = jnp.concatenate([kb[:0:-1], lag0, kf[1:]], axis=0)
    tok = jnp.arange(tc)
    kt = kall[tok[None, :] - tok[:, None] + tc - 1]
    kt = kt.reshape(tc, tc, nq, gl, hh, hh)
    m = jnp.einsum("tsqgab,gk->qtgbska", kt, eye).reshape(nq, S5_LANES, S5_LANES)

    def tile_in(coef):
        c6 = coef.reshape(tc, nq, gl, hh, pp)
        re = jnp.einsum("tqghp,gk->qtghkp", jnp.real(c6), eye).reshape(nq, S5_LANES, gl * pp)
        im = jnp.einsum("tqghp,gk->qtghkp", jnp.imag(c6), eye).reshape(nq, S5_LANES, gl * pp)
        return re, im

    ein_f = jnp.einsum("tgp,gph->tghp", pw[tc - 1::-1, 0][:tc], bb[0])
    ein_b = jnp.einsum("tgp,gph->tghp", pw[:tc, 1], bb[1])
    ein = jnp.concatenate(tile_in(ein_f) + tile_in(ein_b), axis=2)

    def tile_out(coef):
        c6 = coef.reshape(tc, nq, gl, hh, pp)
        re = jnp.einsum("sqgap,gk->qgpska", jnp.real(c6), eye).reshape(nq, gl * pp, S5_LANES)
        im = jnp.einsum("sqgap,gk->qgpska", jnp.imag(c6), eye).reshape(nq, gl * pp, S5_LANES)
        return re, -im

    out_f = jnp.einsum("gap,sgp->sgap", cc[0], pw[1:tc + 1, 0])
    out_b = jnp.einsum("gap,sgp->sgap", cc[1], pw[tc:0:-1, 1])
    gout = jnp.concatenate(tile_out(out_f) + tile_out(out_b), axis=1)

    l16 = pw[tc]
    lam16 = jnp.stack([jnp.real(l16[0]), jnp.imag(l16[0]), jnp.real(l16[1]), jnp.imag(l16[1])], axis=0)
    lam16 = lam16.reshape(4, nq, gl * pp).transpose(1, 0, 2)
    return dict(s5_m=m.astype(BF16), s5_ein=ein.astype(BF16), s5_gout=gout.astype(BF16),
                s5_lam=lam16.astype(F32), s5_d=d.reshape(nq, 1, LANE).astype(F32))


def _fft_dims(l):
    n = 2 * l
    bits = n.bit_length() - 1
    assert n == 1 << bits and bits % 2 == 0, "sequence length must give a square power-of-two DFT size"
    n1 = 1 << (bits // 2)
    hp = -(-(n1 // 2 + 1) // SUBLANE) * SUBLANE
    assert hp % K1_BLOCK == 0
    return n1, n1, hp


def _dft_tables(l):
    n = 2 * l
    n1, n2, hp = _fft_dims(l)
    t2 = jnp.arange(n2, dtype=jnp.int32)[:, None, None]
    k1 = jnp.arange(hp, dtype=jnp.int32)[None, :, None]
    t1 = jnp.arange(n1, dtype=jnp.int32)[None, None, :]
    ang = (2.0 * math.pi / n) * (((t1 * n2 + t2) * k1) % n).astype(F32)
    cos1, sin1 = jnp.cos(ang), jnp.sin(ang)
    e1 = jnp.concatenate([cos1, -sin1], axis=1)
    wgt = jnp.where((k1 == 0) | (k1 == n1 // 2), 1.0, jnp.where(k1 < n1 // 2, 2.0, 0.0))
    half = n1 // 2
    einv = jnp.concatenate([(wgt * cos1)[:, :, :half], (-wgt * sin1)[:, :, :half]], axis=1)
    einv = einv.transpose(0, 2, 1)
    a = jnp.arange(n2, dtype=jnp.int32)
    ang2 = (2.0 * math.pi / n2) * ((a[:, None] * a[None, :]) % n2).astype(F32)
    c2, s2 = jnp.cos(ang2), jnp.sin(ang2)
    f2c = jnp.block([[c2, s2], [-s2, c2]])
    f2ic = jnp.block([[c2, -s2], [s2, c2]])
    return dict(e1=e1.astype(BF16), einv=einv.astype(BF16), f2c=f2c.astype(BF16), f2ic=f2ic.astype(BF16))


def _filter_kernel(w1_ref, b1_ref, w2_ref, b2_ref, sf_ref, om_ref, w3f_ref, w3b_ref, ldf_ref, ldb_ref,
                   e1_ref, f2c_ref, kr_ref, ki_ref, hdn_ref, are_ref, aim_ref, scale_ref, *, l, n1, n2, hp):
    j = pl.program_id(0)
    kb = pl.program_id(1)
    n = 2 * l

    def positions(t2):
        t1 = lax.broadcasted_iota(jnp.int32, (n1, LANE), 0)
        i = t1 * n2 + t2
        return i, jnp.where(i <= l, i, n - i).astype(F32)

    @pl.when((j == 0) & (kb == 0))
    def _():
        def body(t2, carry):
            _, pos = positions(t2)
            ang = pos * om_ref[...]
            lane = lax.broadcasted_iota(jnp.int32, (n1, LANE), 1)
            feats = jnp.where(lane == 0, pos / HYENA_TIME_SCALE,
                              jnp.where(lane <= HYENA_BANDS, jnp.sin(ang),
                                        jnp.where(lane <= 2 * HYENA_BANDS, jnp.cos(ang), 0.0)))
            h1 = jnp.sin(sf_ref[0:1, :] * (jnp.dot(feats, w1_ref[...], preferred_element_type=F32) + b1_ref[...]))
            h2 = jnp.sin(sf_ref[1:2, :] * (jnp.dot(h1, w2_ref[...], preferred_element_type=F32) + b2_ref[...]))
            hdn_ref[pl.ds(pl.multiple_of(t2 * n1, n1), n1), :] = h2
            return carry
        lax.fori_loop(0, n2, body, 0)

    @pl.when(kb == 0)
    def _():
        def body(t2, asum):
            i, pos = positions(t2)
            h2 = hdn_ref[pl.ds(pl.multiple_of(t2 * n1, n1), n1), :]
            t_lin = pos / HYENA_TIME_SCALE
            fwd = jnp.dot(h2, w3f_ref[...], preferred_element_type=F32) * jnp.exp(-t_lin * jnp.exp(ldf_ref[...]))
            bwd = jnp.dot(h2, w3b_ref[...], preferred_element_type=F32) * jnp.exp(-t_lin * jnp.exp(ldb_ref[...]))
            k = jnp.where(i < l, fwd, jnp.where(i > l, bwd, 0.0))
            r = jnp.dot(e1_ref[t2], k.astype(BF16), preferred_element_type=F32)
            a0 = pl.multiple_of(t2 * hp, SUBLANE)
            are_ref[pl.ds(a0, hp), :] = r[:hp]
            aim_ref[pl.ds(a0, hp), :] = r[hp:]
            return asum + jnp.sum(jnp.abs(k), axis=0, keepdims=True)
        asum = lax.fori_loop(0, n2, body, jnp.zeros((1, LANE), F32))
        scale_ref[...] = 1.0 / ((asum + FILTER_EPS) * n)

    for kk in range(K1_BLOCK):
        rows = pl.ds(kb * K1_BLOCK + kk, n2, stride=hp)
        rhs = jnp.concatenate([are_ref[rows, :], aim_ref[rows, :]], axis=0).astype(BF16)
        x = jnp.dot(f2c_ref[...], rhs, preferred_element_type=F32)
        kr_ref[kk * n2:(kk + 1) * n2, :] = x[:n2] * scale_ref[...]
        ki_ref[kk * n2:(kk + 1) * n2, :] = x[n2:] * scale_ref[...]


def _filter_call(l, p, tb):
    n1, n2, hp = _fft_dims(l)
    n = 2 * l
    nt = HYENA_ORDER * D_HYENA // LANE
    per = D_HYENA // LANE
    fwd_col = lambda j, kb: (0, (j // per) * 2 * per + j % per)
    bwd_col = lambda j, kb: (0, (j // per) * 2 * per + per + j % per)
    spec_out = pl.BlockSpec((K1_BLOCK * n2, LANE), lambda j, kb: (kb, j))
    return pl.pallas_call(
        functools.partial(_filter_kernel, l=l, n1=n1, n2=n2, hp=hp),
        out_shape=(jax.ShapeDtypeStruct((hp * n2, nt * LANE), F32),) * 2,
        grid=(nt, hp // K1_BLOCK),
        in_specs=[_resident((LANE, LANE)), _resident((1, LANE)), _resident((LANE, LANE)), _resident((1, LANE)),
                  _resident((2, LANE)), _resident((1, LANE)),
                  pl.BlockSpec((LANE, LANE), fwd_col), pl.BlockSpec((LANE, LANE), bwd_col),
                  pl.BlockSpec((1, LANE), fwd_col), pl.BlockSpec((1, LANE), bwd_col),
                  _resident((n2, 2 * hp, n1)), _resident((2 * n2, 2 * n2))],
        out_specs=(spec_out, spec_out),
        scratch_shapes=[pltpu.VMEM((n, LANE), F32), pltpu.VMEM((n2 * hp, LANE), F32),
                        pltpu.VMEM((n2 * hp, LANE), F32), pltpu.VMEM((1, LANE), F32)],
        compiler_params=_params(("arbitrary", "arbitrary")),
        name="hyena_filter_spectrum",
    )(p["filt_w1"], p["filt_b1"], p["filt_w2"], p["filt_b2"], p["filt_sf"], p["filt_om"],
      p["filt_w3"], p["filt_w3"], p["filt_ld"], p["filt_ld"], tb["e1"], tb["f2c"])


def _conv_kernel(z_ref, g_ref, bias_ref, e1_ref, einv_ref, f2c_ref, f2ic_ref, kr_ref, ki_ref, o_ref,
                 are_ref, aim_ref, *, n1, n2, hp):
    s = pl.program_id(2)
    half = n1 // 2

    @pl.when(s == 0)
    def _():
        def body(t2, carry):
            z = z_ref[pl.ds(t2, half, stride=n2), :].astype(BF16)
            r = jnp.dot(e1_ref[t2, :, 0:half], z, preferred_element_type=F32)
            a0 = pl.multiple_of(t2 * hp, SUBLANE)
            are_ref[pl.ds(a0, hp), :] = r[:hp]
            aim_ref[pl.ds(a0, hp), :] = r[hp:]
            return carry
        lax.fori_loop(0, n2, body, 0)

    for kk in range(K1_BLOCK):
        rows = pl.ds(s * K1_BLOCK + kk, n2, stride=hp)
        rhs = jnp.concatenate([are_ref[rows, :], aim_ref[rows, :]], axis=0).astype(BF16)
        x = jnp.dot(f2c_ref[...], rhs, preferred_element_type=F32)
        xr, xi = x[:n2], x[n2:]
        fr = kr_ref[kk * n2:(kk + 1) * n2, :]
        fi = ki_ref[kk * n2:(kk + 1) * n2, :]
        y = jnp.concatenate([xr * fr - xi * fi, xr * fi + xi * fr], axis=0).astype(BF16)
        back = jnp.dot(f2ic_ref[...], y, preferred_element_type=F32)
        are_ref[rows, :] = back[:n2]
        aim_ref[rows, :] = back[n2:]

    @pl.when(s == pl.num_programs(2) - 1)
    def _():
        def body(t2, carry):
            a0 = pl.multiple_of(t2 * hp, SUBLANE)
            rhs = jnp.concatenate([are_ref[pl.ds(a0, hp), :], aim_ref[pl.ds(a0, hp), :]], axis=0).astype(BF16)
            y = jnp.dot(einv_ref[t2], rhs, preferred_element_type=F32)
            rows = pl.ds(t2, half, stride=n2)
            o_ref[rows, :] = g_ref[rows, :] * (y + bias_ref[...] * z_ref[rows, :])
            return carry
        lax.fori_loop(0, n2, body, 0)


def _conv_call(z, z_col0, g, g_col0, bias, kr, ki, order, tb):
    b, l, _ = z.shape
    n1, n2, hp = _fft_dims(l)
    per = D_HYENA // LANE
    seq = lambda col0: pl.BlockSpec((None, l, LANE), lambda c, i, s: (i, 0, col0 + c))
    spec_k = pl.BlockSpec((K1_BLOCK * n2, LANE), lambda c, i, s: (s, order * per + c))
    return pl.pallas_call(
        functools.partial(_conv_kernel, n1=n1, n2=n2, hp=hp),
        out_shape=jax.ShapeDtypeStruct((b, l, D_HYENA), F32),
        grid=(per, b, hp // K1_BLOCK),
        in_specs=[seq(z_col0), seq(g_col0),
                  pl.BlockSpec((1, LANE), lambda c, i, s: (0, order * per + c)),
                  _resident((n2, 2 * hp, n1)), _resident((n2, n1 // 2, 2 * hp)),
                  _resident((2 * n2, 2 * n2)), _resident((2 * n2, 2 * n2)),
                  spec_k, spec_k],
        out_specs=pl.BlockSpec((None, l, LANE), lambda c, i, s: (i, 0, c)),
        scratch_shapes=[pltpu.VMEM((n2 * hp, LANE), F32), pltpu.VMEM((n2 * hp, LANE), F32)],
        compiler_params=_params(("parallel", "parallel", "arbitrary")),
        name=f"hyena_conv_order{order}",
    )(z, g, bias, tb["e1"], tb["einv"], tb["f2c"], tb["f2ic"], kr, ki)


def _prepare(ffn1_w_gate, ffn1_w_up, ffn1_w_down, ln1_g, ln1_b, w_in,
             ssm_lam_re, ssm_lam_im, ssm_log_step, ssm_b_re, ssm_b_im, ssm_c_re, ssm_c_im,
             ssm_d, ssm_glu_w, ssm_glu_b, ssm_norm_g,
             hy_short_w, hy_short_b, hy_filt_w1, hy_filt_b1, hy_filt_w2, hy_filt_b2, hy_filt_w3,
             hy_sin_freq, hy_log_decay, hy_bias, hy_norm_g, w_out, ln2_g, ln2_b,
             ffn2_w_gate, ffn2_w_up, ffn2_w_down, ln3_g, ln3_b):
    row = lambda a: a[0].reshape(1, -1).astype(F32)
    hid = HYENA_FILTER_HIDDEN
    pad2 = lambda a, r, c: jnp.zeros((r, c), F32).at[:a.shape[0], :a.shape[1]].set(a.astype(F32))
    omega = jnp.exp(-math.log(HYENA_MAX_PERIOD) * jnp.arange(HYENA_BANDS, dtype=F32) / HYENA_BANDS)
    om = jnp.zeros((1, LANE), F32).at[0, 1:1 + HYENA_BANDS].set(omega).at[0, 1 + HYENA_BANDS:HYENA_POS_DIM].set(omega)
    p = dict(
        ffn1_wg=ffn1_w_gate[0].astype(BF16), ffn1_wu=ffn1_w_up[0].astype(BF16), ffn1_wd=ffn1_w_down[0].astype(BF16),
        ln1_g=row(ln1_g), ln1_b=row(ln1_b), w_in=w_in[0].astype(BF16),
        glu_w=ssm_glu_w[0].astype(BF16), glu_b=row(ssm_glu_b), ssm_norm_g=row(ssm_norm_g),
        hy_norm_g=row(hy_norm_g), w_out=w_out[0].astype(BF16), ln2_g=row(ln2_g), ln2_b=row(ln2_b),
        ffn2_wg=ffn2_w_gate[0].astype(BF16), ffn2_wu=ffn2_w_up[0].astype(BF16), ffn2_wd=ffn2_w_down[0].astype(BF16),
        ln3_g=row(ln3_g), ln3_b=row(ln3_b),
        short_w=hy_short_w[0].astype(F32), short_b=row(hy_short_b),
        filt_w1=pad2(hy_filt_w1[0], LANE, LANE), filt_b1=pad2(hy_filt_b1[0].reshape(1, hid), 1, LANE),
        filt_w2=pad2(hy_filt_w2[0], LANE, LANE), filt_b2=pad2(hy_filt_b2[0].reshape(1, hid), 1, LANE),
        filt_sf=pad2(hy_sin_freq[0], 2, LANE), filt_om=om,
        filt_w3=pad2(hy_filt_w3[0], LANE, hy_filt_w3.shape[-1]), filt_ld=row(hy_log_decay),
        hy_bias=hy_bias[0].reshape(1, HYENA_ORDER * D_HYENA).astype(F32),
    )
    f32 = lambda a: a[0].astype(F32)
    p.update(_s5_tables(f32(ssm_lam_re), f32(ssm_lam_im), f32(ssm_log_step), f32(ssm_b_re), f32(ssm_b_im),
                        f32(ssm_c_re), f32(ssm_c_im), f32(ssm_d)))
    return p


def _trunk(x, p):
    b, l, d = x.shape
    t = b * l
    per = D_HYENA // LANE
    h1, u, hy_raw = _ffn1_call(x.reshape(t, d), p)
    hy = _shortconv_call(hy_raw.reshape(b, l, D_HY_IN), p)
    y_ssm = _s5_call(u.reshape(b, l, D_SSM), p)
    tb = _dft_tables(l)
    kr, ki = _filter_call(l, p, tb)
    z1 = _conv_call(hy, 0, hy, per, p["hy_bias"], kr, ki, 0, tb)
    z2 = _conv_call(z1, 0, hy, 2 * per, p["hy_bias"], kr, ki, 1, tb)
    out = _final_call(y_ssm.reshape(t, D_SSM), z2.reshape(t, D_HYENA), h1, p)
    return out.reshape(b, l, d)


def kernel(x_prompt, x_sample, ffn1_w_gate, ffn1_w_up, ffn1_w_down, ln1_g, ln1_b, w_in, ssm_lam_re, ssm_lam_im, ssm_log_step, ssm_b_re, ssm_b_im, ssm_c_re, ssm_c_im, ssm_d, ssm_glu_w, ssm_glu_b, ssm_norm_g, hy_short_w, hy_short_b, hy_filt_w1, hy_filt_b1, hy_filt_w2, hy_filt_b2, hy_filt_w3, hy_sin_freq, hy_log_decay, hy_bias, hy_norm_g, w_out, ln2_g, ln2_b, ffn2_w_gate, ffn2_w_up, ffn2_w_down, ln3_g, ln3_b):
    p = _prepare(ffn1_w_gate, ffn1_w_up, ffn1_w_down, ln1_g, ln1_b, w_in,
                 ssm_lam_re, ssm_lam_im, ssm_log_step, ssm_b_re, ssm_b_im, ssm_c_re, ssm_c_im,
                 ssm_d, ssm_glu_w, ssm_glu_b, ssm_norm_g,
                 hy_short_w, hy_short_b, hy_filt_w1, hy_filt_b1, hy_filt_w2, hy_filt_b2, hy_filt_w3,
                 hy_sin_freq, hy_log_decay, hy_bias, hy_norm_g, w_out, ln2_g, ln2_b,
                 ffn2_w_gate, ffn2_w_up, ffn2_w_down, ln3_g, ln3_b)
    return (_trunk(x_prompt, p), _trunk(x_sample, p))
```

```python
import functools
import math

import jax
import jax.numpy as jnp
from jax import lax
from jax.experimental import pallas as pl
from jax.experimental.pallas import tpu as pltpu

F32 = jnp.float32
BF16 = jnp.bfloat16

D_MODEL = 1024
D_SSM = 512
SSM_GROUP = 16
N_SSM_GROUPS = D_SSM // SSM_GROUP
SSM_STATE = 64
D_HYENA = 512
HYENA_ORDER = 2
HYENA_BANDS = 8
HYENA_POS_DIM = 1 + 2 * HYENA_BANDS
HYENA_FILTER_HIDDEN = 64
HYENA_TIME_SCALE = 4096.0
HYENA_MAX_PERIOD = 10000.0
D_HY_IN = (HYENA_ORDER + 1) * D_HYENA
D_IN = D_SSM + D_HY_IN
D_FF = 128 * math.ceil(8 * D_MODEL / 3 / 128)
LN_EPS = 1e-5
RMS_EPS = 1e-6
FILTER_EPS = 1e-6
DEPTH = 1
DEEPNORM_ALPHA = (2.0 * DEPTH) ** 0.25

LANE = 128
SUBLANE = 8
V7X_VMEM_BYTES = 64 * 1024 * 1024
VMEM_LIMIT_BYTES = V7X_VMEM_BYTES - 8 * 1024 * 1024

TOKEN_TILE = 512
FF_CHUNK = 256
CONV_TILE = 512
S5_CHUNK = 16
S5_LANES = S5_CHUNK * LANE
S5_STATE_LANES = (LANE // SSM_GROUP) * SSM_STATE
K1_BLOCK = 8
STAGE_UNROLL = 8


def _params(semantics):
    return pltpu.CompilerParams(dimension_semantics=semantics, vmem_limit_bytes=VMEM_LIMIT_BYTES)


def _resident(shape):
    nd = len(shape)
    return pl.BlockSpec(shape, lambda *_: (0,) * nd, pipeline_mode=pl.Buffered(1))


def _layer_norm(x, g, b):
    mu = jnp.mean(x, axis=-1, keepdims=True)
    xc = x - mu
    var = jnp.mean(xc * xc, axis=-1, keepdims=True)
    return xc * lax.rsqrt(var + LN_EPS) * g + b


def _rms_norm(x, g):
    ms = jnp.mean(x * x, axis=-1, keepdims=True)
    return x * lax.rsqrt(ms + RMS_EPS) * g


def _ffn_ln(x, wg_ref, wu_ref, wd_ref, g_ref, b_ref, acc_ref):
    xb = x.astype(BF16)
    acc_ref[...] = jnp.zeros_like(acc_ref)

    def body(c, carry):
        off = pl.multiple_of(c * FF_CHUNK, FF_CHUNK)
        gate = jnp.dot(xb, wg_ref[:, pl.ds(off, FF_CHUNK)], preferred_element_type=F32)
        up = jnp.dot(xb, wu_ref[:, pl.ds(off, FF_CHUNK)], preferred_element_type=F32)
        act = (jax.nn.silu(gate) * up).astype(BF16)
        acc_ref[...] += jnp.dot(act, wd_ref[pl.ds(off, FF_CHUNK), :], preferred_element_type=F32)
        return carry

    lax.fori_loop(0, D_FF // FF_CHUNK, body, 0)
    return _layer_norm(DEEPNORM_ALPHA * x + 0.5 * acc_ref[...], g_ref[...], b_ref[...])


def _ffn1_kernel(x_ref, wg_ref, wu_ref, wd_ref, g_ref, b_ref, win_ref, h_ref, u_ref, hy_ref, acc_ref):
    h = _ffn_ln(x_ref[...], wg_ref, wu_ref, wd_ref, g_ref, b_ref, acc_ref)
    h_ref[...] = h
    hb = h.astype(BF16)
    u_ref[...] = jnp.dot(hb, win_ref[:, :D_SSM], preferred_element_type=F32)
    hy_ref[...] = jnp.dot(hb, win_ref[:, D_SSM:], preferred_element_type=F32)


def _ffn1_call(x2d, p):
    t = x2d.shape[0]
    tm = TOKEN_TILE
    row = lambda w: pl.BlockSpec((tm, w), lambda i: (i, 0))
    return pl.pallas_call(
        _ffn1_kernel,
        out_shape=(jax.ShapeDtypeStruct((t, D_MODEL), F32),
                   jax.ShapeDtypeStruct((t, D_SSM), F32),
                   jax.ShapeDtypeStruct((t, D_HY_IN), F32)),
        grid=(t // tm,),
        in_specs=[row(D_MODEL), _resident((D_MODEL, D_FF)), _resident((D_MODEL, D_FF)),
                  _resident((D_FF, D_MODEL)), _resident((1, D_MODEL)), _resident((1, D_MODEL)),
                  _resident((D_MODEL, D_IN))],
        out_specs=(row(D_MODEL), row(D_SSM), row(D_HY_IN)),
        scratch_shapes=[pltpu.VMEM((tm, D_MODEL), F32)],
        compiler_params=_params(("parallel",)),
        name="ffn1_ln1_proj",
    )(x2d, p["ffn1_wg"], p["ffn1_wu"], p["ffn1_wd"], p["ln1_g"], p["ln1_b"], p["w_in"])


def _final_kernel(ys_ref, yh_ref, h_ref, gw_ref, gb_ref, sg_ref, hg_ref, wo_ref, l2g_ref, l2b_ref,
                  wg_ref, wu_ref, wd_ref, l3g_ref, l3b_ref, o_ref, acc_ref):
    g = jax.nn.gelu(ys_ref[...])
    gate = jax.nn.sigmoid(jnp.dot(g.astype(BF16), gw_ref[...], preferred_element_type=F32) + gb_ref[...])
    y_ssm = _rms_norm(g * gate, sg_ref[...])
    y_hy = _rms_norm(yh_ref[...], hg_ref[...])
    mix = (jnp.dot(y_ssm.astype(BF16), wo_ref[:D_SSM, :], preferred_element_type=F32)
           + jnp.dot(y_hy.astype(BF16), wo_ref[D_SSM:, :], preferred_element_type=F32))
    x2 = _layer_norm(DEEPNORM_ALPHA * h_ref[...] + mix, l2g_ref[...], l2b_ref[...])
    o_ref[...] = _ffn_ln(x2, wg_ref, wu_ref, wd_ref, l3g_ref, l3b_ref, acc_ref)


def _final_call(y_ssm, y_hy, h1, p):
    t = h1.shape[0]
    tm = TOKEN_TILE
    row = lambda w: pl.BlockSpec((tm, w), lambda i: (i, 0))
    return pl.pallas_call(
        _final_kernel,
        out_shape=jax.ShapeDtypeStruct((t, D_MODEL), F32),
        grid=(t // tm,),
        in_specs=[row(D_SSM), row(D_HYENA), row(D_MODEL),
                  _resident((D_SSM, D_SSM)), _resident((1, D_SSM)), _resident((1, D_SSM)),
                  _resident((1, D_HYENA)), _resident((D_MODEL, D_MODEL)),
                  _resident((1, D_MODEL)), _resident((1, D_MODEL)),
                  _resident((D_MODEL, D_FF)), _resident((D_MODEL, D_FF)), _resident((D_FF, D_MODEL)),
                  _resident((1, D_MODEL)), _resident((1, D_MODEL))],
        out_specs=row(D_MODEL),
        scratch_shapes=[pltpu.VMEM((tm, D_MODEL), F32)],
        compiler_params=_params(("parallel",)),
        name="mix_ln2_ffn2_ln3",
    )(y_ssm, y_hy, h1, p["glu_w"], p["glu_b"], p["ssm_norm_g"], p["hy_norm_g"], p["w_out"],
      p["ln2_g"], p["ln2_b"], p["ffn2_wg"], p["ffn2_wu"], p["ffn2_wd"], p["ln3_g"], p["ln3_b"])


def _shortconv_kernel(x_ref, prev_ref, next_ref, w_ref, b_ref, o_ref):
    j = pl.program_id(1)
    x = x_ref[...]
    rows = x.shape[0]
    row = lax.broadcasted_iota(jnp.int32, x.shape, 0)
    before = jnp.where(j > 0, prev_ref[SUBLANE - 1:SUBLANE, :], 0.0)
    after = jnp.where(j < pl.num_programs(1) - 1, next_ref[0:1, :], 0.0)
    x_m1 = jnp.where(row == 0, before, pltpu.roll(x, 1, 0))
    x_p1 = jnp.where(row == rows - 1, after, pltpu.roll(x, rows - 1, 0))
    o_ref[...] = ((b_ref[...] + x_m1 * w_ref[0:1, :]) + x * w_ref[1:2, :]) + x_p1 * w_ref[2:3, :]


def _shortconv_call(hy_raw, p):
    b, l, c = hy_raw.shape
    tt = CONV_TILE
    per = tt // SUBLANE
    last = l // SUBLANE - 1
    return pl.pallas_call(
        _shortconv_kernel,
        out_shape=jax.ShapeDtypeStruct((b, l, c), F32),
        grid=(b, l // tt),
        in_specs=[pl.BlockSpec((None, tt, c), lambda i, j: (i, j, 0)),
                  pl.BlockSpec((None, SUBLANE, c), lambda i, j: (i, jnp.maximum(j * per - 1, 0), 0)),
                  pl.BlockSpec((None, SUBLANE, c), lambda i, j: (i, jnp.minimum((j + 1) * per, last), 0)),
                  _resident((3, c)), _resident((1, c))],
        out_specs=pl.BlockSpec((None, tt, c), lambda i, j: (i, j, 0)),
        compiler_params=_params(("parallel", "parallel")),
        name="hyena_shortconv",
    )(hy_raw, hy_raw, hy_raw, p["short_w"], p["short_b"])


def _s5_kernel(u_ref, m_ref, ein_ref, gout_ref, lam_ref, d_ref, y_ref, ucat_ref, st_ref, *, nch, rb):
    sl = S5_STATE_LANES
    for t in range(S5_CHUNK):
        ucat_ref[:, t * LANE:(t + 1) * LANE] = u_ref[pl.ds(t, nch, stride=S5_CHUNK), :].astype(BF16)

    def state_in(i, carry):
        r0 = pl.multiple_of(i * rb, rb)
        lhs = ucat_ref[pl.ds(r0, rb), :]
        for cb in range(4):
            st_ref[pl.ds(r0, rb), cb * sl:(cb + 1) * sl] = jnp.dot(
                lhs, ein_ref[:, cb * sl:(cb + 1) * sl], preferred_element_type=F32)
        return carry

    lax.fori_loop(0, nch // rb, state_in, 0)

    lfr, lfi = lam_ref[0:1, :], lam_ref[1:2, :]
    lbr, lbi = lam_ref[2:3, :], lam_ref[3:4, :]

    def scan(i, carry):
        sfr, sfi, sbr, sbi = carry
        j = nch - 1 - i
        xfr = st_ref[pl.ds(i, 1), 0:sl]
        xfi = st_ref[pl.ds(i, 1), sl:2 * sl]
        st_ref[pl.ds(i, 1), 0:sl] = sfr
        st_ref[pl.ds(i, 1), sl:2 * sl] = sfi
        xbr = st_ref[pl.ds(j, 1), 2 * sl:3 * sl]
        xbi = st_ref[pl.ds(j, 1), 3 * sl:4 * sl]
        st_ref[pl.ds(j, 1), 2 * sl:3 * sl] = sbr
        st_ref[pl.ds(j, 1), 3 * sl:4 * sl] = sbi
        return (lfr * sfr - lfi * sfi + xfr, lfr * sfi + lfi * sfr + xfi,
                lbr * sbr - lbi * sbi + xbr, lbr * sbi + lbi * sbr + xbi)

    zero = jnp.zeros((1, sl), F32)
    lax.fori_loop(0, nch, scan, (zero, zero, zero, zero))

    def emit(i, carry):
        r0 = pl.multiple_of(i * rb, rb)
        lhs_u = ucat_ref[pl.ds(r0, rb), :]
        lhs_s = st_ref[pl.ds(r0, rb), :].astype(BF16)
        for cb in range(4):
            cols = slice(cb * 4 * LANE, (cb + 1) * 4 * LANE)
            yc = (jnp.dot(lhs_u, m_ref[:, cols], preferred_element_type=F32)
                  + jnp.dot(lhs_s, gout_ref[:, cols], preferred_element_type=F32))
            for tt in range(4):
                rows = pl.ds(r0 * S5_CHUNK + cb * 4 + tt, rb, stride=S5_CHUNK)
                y_ref[rows, :] = yc[:, tt * LANE:(tt + 1) * LANE] + d_ref[...] * u_ref[rows, :]
        return carry

    lax.fori_loop(0, nch // rb, emit, 0)


def _s5_call(u, p):
    b, l, _ = u.shape
    nch = l // S5_CHUNK
    rb = min(nch, 256)
    nq = D_SSM // LANE
    mat = pl.BlockSpec((None, S5_LANES, S5_LANES), lambda q, i: (q, 0, 0), pipeline_mode=pl.Buffered(1))
    return pl.pallas_call(
        functools.partial(_s5_kernel, nch=nch, rb=rb),
        out_shape=jax.ShapeDtypeStruct((b, l, D_SSM), F32),
        grid=(nq, b),
        in_specs=[pl.BlockSpec((None, l, LANE), lambda q, i: (i, 0, q)),
                  mat, mat, mat,
                  pl.BlockSpec((None, 4, S5_STATE_LANES), lambda q, i: (q, 0, 0)),
                  pl.BlockSpec((None, 1, LANE), lambda q, i: (q, 0, 0))],
        out_specs=pl.BlockSpec((None, l, LANE), lambda q, i: (i, 0, q)),
        scratch_shapes=[pltpu.VMEM((nch, S5_LANES), BF16), pltpu.VMEM((nch, 4 * S5_STATE_LANES), F32)],
        compiler_params=_params(("parallel", "parallel")),
        name="s5_chunked",
    )(u, p["s5_m"], p["s5_ein"], p["s5_gout"], p["s5_lam"], p["s5_d"])


def _cmul(ar, ai, br, bi):
    return ar * br - ai * bi, ar * bi + ai * br


def _s5_tables(lam_re, lam_im, log_step, b_re, b_im, c_re, c_im, d):
    g, hh, pp, tc = N_SSM_GROUPS, SSM_GROUP, SSM_STATE, S5_CHUNK
    gl = LANE // hh
    nq = g // gl
    step = jnp.exp(log_step)[..., None]
    mag = jnp.exp(lam_re * step)
    ar = mag * jnp.cos(lam_im * step)
    ai = mag * jnp.sin(lam_im * step)
    nr, ni = ar - 1.0, ai
    den = lam_re * lam_re + lam_im * lam_im
    qr = (nr * lam_re + ni * lam_im) / den
    qi = (ni * lam_re - nr * lam_im) / den
    bbr = qr[..., None] * b_re - qi[..., None] * b_im
    bbi = qr[..., None] * b_im + qi[..., None] * b_re
    pr, pi = [jnp.ones_like(ar)], [jnp.zeros_like(ai)]
    for _ in range(tc):
        r, i = _cmul(pr[-1], pi[-1], ar, ai)
        pr.append(r)
        pi.append(i)
    pwr, pwi = jnp.stack(pr), jnp.stack(pi)
    eye = jnp.eye(gl, dtype=F32)
    exact = lax.Precision.HIGHEST

    def c_times_pow(dd, sel):
        return _cmul(c_re[dd][None], c_im[dd][None], pwr[sel, dd][:, :, None, :], pwi[sel, dd][:, :, None, :])

    def lag_kernels(dd):
        zr, zi = c_times_pow(dd, jnp.arange(tc))
        return (jnp.einsum("jgap,gpb->jgab", zr, bbr[dd], precision=exact)
                - jnp.einsum("jgap,gpb->jgab", zi, bbi[dd], precision=exact))

    kf, kb = lag_kernels(0), lag_kernels(1)
    kall = jnp.concatenate([kb[jnp.arange(tc - 1, 0, -1)], kf[:1] + kb[:1], kf[1:]], axis=0)
    k6 = kall.reshape(2 * tc - 1, nq, gl, hh, hh).transpose(0, 1, 2, 4, 3)
    blk = (k6[:, :, :, :, None, :] * eye[:, None, :, None]).reshape(2 * tc - 1, nq, LANE, LANE).astype(BF16)
    tok = jnp.arange(tc)
    m = blk[tok[None, :] - tok[:, None] + tc - 1]
    m = m.transpose(2, 0, 3, 1, 4).reshape(nq, S5_LANES, S5_LANES)

    eye16 = eye.astype(BF16)

    def tile_in(coef):
        c6 = coef.astype(BF16).reshape(tc, nq, gl, pp, hh).transpose(1, 0, 2, 4, 3)
        return (c6[:, :, :, :, None, :] * eye16[:, None, :, None]).reshape(nq, S5_LANES, gl * pp)

    def state_in(dd, sel):
        return _cmul(pwr[sel, dd][..., None], pwi[sel, dd][..., None], bbr[dd][None], bbi[dd][None])

    ein = jnp.concatenate([tile_in(c) for c in state_in(0, jnp.arange(tc - 1, -1, -1))]
                          + [tile_in(c) for c in state_in(1, jnp.arange(tc))], axis=2)

    def tile_out(coef):
        c6 = coef.astype(BF16).reshape(tc, nq, gl, hh, pp).transpose(1, 2, 4, 0, 3)
        return (c6[:, :, :, :, None, :] * eye16[:, None, None, :, None]).reshape(nq, gl * pp, S5_LANES)

    ofr, ofi = c_times_pow(0, jnp.arange(1, tc + 1))
    obr, obi = c_times_pow(1, jnp.arange(tc, 0, -1))
    gout = jnp.concatenate([tile_out(ofr), tile_out(-ofi), tile_out(obr), tile_out(-obi)], axis=1)

    lam16 = jnp.stack([pwr[tc, 0], pwi[tc, 0], pwr[tc, 1], pwi[tc, 1]], axis=0)
    lam16 = lam16.reshape(4, nq, gl * pp).transpose(1, 0, 2)
    return dict(s5_m=m, s5_ein=ein, s5_gout=gout,
                s5_lam=lam16.astype(F32), s5_d=d.reshape(nq, 1, LANE).astype(F32))


def _fft_dims(l):
    n = 2 * l
    bits = n.bit_length() - 1
    assert n == 1 << bits and bits % 2 == 0, "sequence length must give a square power-of-two DFT size"
    n1 = 1 << (bits // 2)
    hp = -(-(n1 // 2 + 1) // SUBLANE) * SUBLANE
    assert hp % K1_BLOCK == 0
    return n1, n1, hp


def _dft_tables(l):
    n = 2 * l
    n1, n2, hp = _fft_dims(l)
    t2 = jnp.arange(n2, dtype=jnp.int32)[:, None, None]
    k1 = jnp.arange(hp, dtype=jnp.int32)[None, :, None]
    t1 = jnp.arange(n1, dtype=jnp.int32)[None, None, :]
    ang = (2.0 * math.pi / n) * (((t1 * n2 + t2) * k1) % n).astype(F32)
    cos1, sin1 = jnp.cos(ang), jnp.sin(ang)
    e1 = jnp.concatenate([cos1, -sin1], axis=1)
    wgt = jnp.where((k1 == 0) | (k1 == n1 // 2), 1.0, jnp.where(k1 < n1 // 2, 2.0, 0.0))
    half = n1 // 2
    einv = jnp.concatenate([(wgt * cos1)[:, :, :half], (-wgt * sin1)[:, :, :half]], axis=1)
    einv = einv.transpose(0, 2, 1)
    a = jnp.arange(n2, dtype=jnp.int32)
    ang2 = (2.0 * math.pi / n2) * ((a[:, None] * a[None, :]) % n2).astype(F32)
    c2, s2 = jnp.cos(ang2), jnp.sin(ang2)
    f2c = jnp.block([[c2, s2], [-s2, c2]])
    f2ic = jnp.block([[c2, -s2], [s2, c2]])
    return dict(e1=e1.astype(BF16), einv=einv.astype(BF16), f2c=f2c.astype(BF16), f2ic=f2ic.astype(BF16))


def _filter_kernel(w1_ref, b1_ref, w2_ref, b2_ref, sf_ref, om_ref, w3f_ref, w3b_ref, ldf_ref, ldb_ref,
                   e1_ref, f2c_ref, kr_ref, ki_ref, hdn_ref, are_ref, aim_ref, scale_ref, *, l, n1, n2, hp):
    j = pl.program_id(0)
    kb = pl.program_id(1)
    n = 2 * l

    def positions(t2):
        t1 = lax.broadcasted_iota(jnp.int32, (n1, LANE), 0)
        i = t1 * n2 + t2
        return i, jnp.where(i <= l, i, n - i).astype(F32)

    @pl.when((j == 0) & (kb == 0))
    def _():
        def body(t2, carry):
            _, pos = positions(t2)
            ang = pos * om_ref[...]
            lane = lax.broadcasted_iota(jnp.int32, (n1, LANE), 1)
            feats = jnp.where(lane == 0, pos / HYENA_TIME_SCALE,
                              jnp.where(lane <= HYENA_BANDS, jnp.sin(ang),
                                        jnp.where(lane <= 2 * HYENA_BANDS, jnp.cos(ang), 0.0)))
            h1 = jnp.sin(sf_ref[0:1, :] * (jnp.dot(feats, w1_ref[...], preferred_element_type=F32) + b1_ref[...]))
            h2 = jnp.sin(sf_ref[1:2, :] * (jnp.dot(h1, w2_ref[...], preferred_element_type=F32) + b2_ref[...]))
            hdn_ref[pl.ds(pl.multiple_of(t2 * n1, n1), n1), :] = h2
            return carry
        lax.fori_loop(0, n2, body, 0)

    @pl.when(kb == 0)
    def _():
        def body(t2, asum):
            i, pos = positions(t2)
            h2 = hdn_ref[pl.ds(pl.multiple_of(t2 * n1, n1), n1), :]
            t_lin = pos / HYENA_TIME_SCALE
            fwd = jnp.dot(h2, w3f_ref[...], preferred_element_type=F32) * jnp.exp(-t_lin * jnp.exp(ldf_ref[...]))
            bwd = jnp.dot(h2, w3b_ref[...], preferred_element_type=F32) * jnp.exp(-t_lin * jnp.exp(ldb_ref[...]))
            k = jnp.where(i < l, fwd, jnp.where(i > l, bwd, 0.0))
            r = jnp.dot(e1_ref[t2], k.astype(BF16), preferred_element_type=F32)
            a0 = pl.multiple_of(t2 * hp, SUBLANE)
            are_ref[pl.ds(a0, hp), :] = r[:hp]
            aim_ref[pl.ds(a0, hp), :] = r[hp:]
            return asum + jnp.sum(jnp.abs(k), axis=0, keepdims=True)
        asum = lax.fori_loop(0, n2, body, jnp.zeros((1, LANE), F32), unroll=STAGE_UNROLL)
        scale_ref[...] = 1.0 / ((asum + FILTER_EPS) * n)

    for kk in range(K1_BLOCK):
        rows = pl.ds(kb * K1_BLOCK + kk, n2, stride=hp)
        rhs = jnp.concatenate([are_ref[rows, :], aim_ref[rows, :]], axis=0).astype(BF16)
        x = jnp.dot(f2c_ref[...], rhs, preferred_element_type=F32)
        kr_ref[kk * n2:(kk + 1) * n2, :] = x[:n2] * scale_ref[...]
        ki_ref[kk * n2:(kk + 1) * n2, :] = x[n2:] * scale_ref[...]


def _filter_call(l, p, tb):
    n1, n2, hp = _fft_dims(l)
    n = 2 * l
    nt = HYENA_ORDER * D_HYENA // LANE
    per = D_HYENA // LANE
    fwd_col = lambda j, kb: (0, (j // per) * 2 * per + j % per)
    bwd_col = lambda j, kb: (0, (j // per) * 2 * per + per + j % per)
    spec_out = pl.BlockSpec((K1_BLOCK * n2, LANE), lambda j, kb: (kb, j))
    return pl.pallas_call(
        functools.partial(_filter_kernel, l=l, n1=n1, n2=n2, hp=hp),
        out_shape=(jax.ShapeDtypeStruct((hp * n2, nt * LANE), F32),) * 2,
        grid=(nt, hp // K1_BLOCK),
        in_specs=[_resident((LANE, LANE)), _resident((1, LANE)), _resident((LANE, LANE)), _resident((1, LANE)),
                  _resident((2, LANE)), _resident((1, LANE)),
                  pl.BlockSpec((LANE, LANE), fwd_col), pl.BlockSpec((LANE, LANE), bwd_col),
                  pl.BlockSpec((1, LANE), fwd_col), pl.BlockSpec((1, LANE), bwd_col),
                  _resident((n2, 2 * hp, n1)), _resident((2 * n2, 2 * n2))],
        out_specs=(spec_out, spec_out),
        scratch_shapes=[pltpu.VMEM((n, LANE), F32), pltpu.VMEM((n2 * hp, LANE), F32),
                        pltpu.VMEM((n2 * hp, LANE), F32), pltpu.VMEM((1, LANE), F32)],
        compiler_params=_params(("arbitrary", "arbitrary")),
        name="hyena_filter_spectrum",
    )(p["filt_w1"], p["filt_b1"], p["filt_w2"], p["filt_b2"], p["filt_sf"], p["filt_om"],
      p["filt_w3"], p["filt_w3"], p["filt_ld"], p["filt_ld"], tb["e1"], tb["f2c"])


def _conv_kernel(z_ref, g_ref, bias_ref, e1_ref, einv_ref, f2c_ref, f2ic_ref, kr_ref, ki_ref, o_ref,
                 are_ref, aim_ref, *, n1, n2, hp):
    s = pl.program_id(2)
    half = n1 // 2

    @pl.when(s == 0)
    def _():
        def body(t2, carry):
            z = z_ref[pl.ds(t2, half, stride=n2), :].astype(BF16)
            r = jnp.dot(e1_ref[t2, :, 0:half], z, preferred_element_type=F32)
            a0 = pl.multiple_of(t2 * hp, SUBLANE)
            are_ref[pl.ds(a0, hp), :] = r[:hp]
            aim_ref[pl.ds(a0, hp), :] = r[hp:]
            return carry
        lax.fori_loop(0, n2, body, 0, unroll=STAGE_UNROLL)

    for kk in range(0, K1_BLOCK, 2):
        rows = [pl.ds(s * K1_BLOCK + kk + e, n2, stride=hp) for e in range(2)]
        pair = lambda ref: jnp.concatenate([ref[rows[0], :], ref[rows[1], :]], axis=1)
        rhs = jnp.concatenate([pair(are_ref), pair(aim_ref)], axis=0).astype(BF16)
        x = jnp.dot(f2c_ref[...], rhs, preferred_element_type=F32)
        xr, xi = x[:n2], x[n2:]
        spec = lambda ref: jnp.concatenate([ref[(kk + e) * n2:(kk + e + 1) * n2, :] for e in range(2)], axis=1)
        fr, fi = spec(kr_ref), spec(ki_ref)
        y = jnp.concatenate([xr * fr - xi * fi, xr * fi + xi * fr], axis=0).astype(BF16)
        back = jnp.dot(f2ic_ref[...], y, preferred_element_type=F32)
        for e in range(2):
            are_ref[rows[e], :] = back[:n2, e * LANE:(e + 1) * LANE]
            aim_ref[rows[e], :] = back[n2:, e * LANE:(e + 1) * LANE]

    @pl.when(s == pl.num_programs(2) - 1)
    def _():
        def body(t2, carry):
            a0 = pl.multiple_of(t2 * hp, SUBLANE)
            rhs = jnp.concatenate([are_ref[pl.ds(a0, hp), :], aim_ref[pl.ds(a0, hp), :]], axis=0).astype(BF16)
            y = jnp.dot(einv_ref[t2], rhs, preferred_element_type=F32)
            rows = pl.ds(t2, half, stride=n2)
            o_ref[rows, :] = g_ref[rows, :] * (y + bias_ref[...] * z_ref[rows, :])
            return carry
        lax.fori_loop(0, n2, body, 0, unroll=STAGE_UNROLL)


def _conv_call(z, z_col0, g, g_col0, bias, kr, ki, order, tb):
    b, l, _ = z.shape
    n1, n2, hp = _fft_dims(l)
    per = D_HYENA // LANE
    seq = lambda col0: pl.BlockSpec((None, l, LANE), lambda c, i, s: (i, 0, col0 + c))
    spec_k = pl.BlockSpec((K1_BLOCK * n2, LANE), lambda c, i, s: (s, order * per + c))
    return pl.pallas_call(
        functools.partial(_conv_kernel, n1=n1, n2=n2, hp=hp),
        out_shape=jax.ShapeDtypeStruct((b, l, D_HYENA), F32),
        grid=(per, b, hp // K1_BLOCK),
        in_specs=[seq(z_col0), seq(g_col0),
                  pl.BlockSpec((1, LANE), lambda c, i, s: (0, order * per + c)),
                  _resident((n2, 2 * hp, n1)), _resident((n2, n1 // 2, 2 * hp)),
                  _resident((2 * n2, 2 * n2)), _resident((2 * n2, 2 * n2)),
                  spec_k, spec_k],
        out_specs=pl.BlockSpec((None, l, LANE), lambda c, i, s: (i, 0, c)),
        scratch_shapes=[pltpu.VMEM((n2 * hp, LANE), F32), pltpu.VMEM((n2 * hp, LANE), F32)],
        compiler_params=_params(("parallel", "parallel", "arbitrary")),
        name=f"hyena_conv_order{order}",
    )(z, g, bias, tb["e1"], tb["einv"], tb["f2c"], tb["f2ic"], kr, ki)


def _prepare(ffn1_w_gate, ffn1_w_up, ffn1_w_down, ln1_g, ln1_b, w_in,
             ssm_lam_re, ssm_lam_im, ssm_log_step, ssm_b_re, ssm_b_im, ssm_c_re, ssm_c_im,
             ssm_d, ssm_glu_w, ssm_glu_b, ssm_norm_g,
             hy_short_w, hy_short_b, hy_filt_w1, hy_filt_b1, hy_filt_w2, hy_filt_b2, hy_filt_w3,
             hy_sin_freq, hy_log_decay, hy_bias, hy_norm_g, w_out, ln2_g, ln2_b,
             ffn2_w_gate, ffn2_w_up, ffn2_w_down, ln3_g, ln3_b):
    row = lambda a: a[0].reshape(1, -1).astype(F32)
    hid = HYENA_FILTER_HIDDEN
    pad2 = lambda a, r, c: jnp.zeros((r, c), F32).at[:a.shape[0], :a.shape[1]].set(a.astype(F32))
    omega = jnp.exp(-math.log(HYENA_MAX_PERIOD) * jnp.arange(HYENA_BANDS, dtype=F32) / HYENA_BANDS)
    om = jnp.zeros((1, LANE), F32).at[0, 1:1 + HYENA_BANDS].set(omega).at[0, 1 + HYENA_BANDS:HYENA_POS_DIM].set(omega)
    p = dict(
        ffn1_wg=ffn1_w_gate[0].astype(BF16), ffn1_wu=ffn1_w_up[0].astype(BF16), ffn1_wd=ffn1_w_down[0].astype(BF16),
        ln1_g=row(ln1_g), ln1_b=row(ln1_b), w_in=w_in[0].astype(BF16),
        glu_w=ssm_glu_w[0].astype(BF16), glu_b=row(ssm_glu_b), ssm_norm_g=row(ssm_norm_g),
        hy_norm_g=row(hy_norm_g), w_out=w_out[0].astype(BF16), ln2_g=row(ln2_g), ln2_b=row(ln2_b),
        ffn2_wg=ffn2_w_gate[0].astype(BF16), ffn2_wu=ffn2_w_up[0].astype(BF16), ffn2_wd=ffn2_w_down[0].astype(BF16),
        ln3_g=row(ln3_g), ln3_b=row(ln3_b),
        short_w=hy_short_w[0].astype(F32), short_b=row(hy_short_b),
        filt_w1=pad2(hy_filt_w1[0], LANE, LANE), filt_b1=pad2(hy_filt_b1[0].reshape(1, hid), 1, LANE),
        filt_w2=pad2(hy_filt_w2[0], LANE, LANE), filt_b2=pad2(hy_filt_b2[0].reshape(1, hid), 1, LANE),
        filt_sf=pad2(hy_sin_freq[0], 2, LANE), filt_om=om,
        filt_w3=pad2(hy_filt_w3[0], LANE, hy_filt_w3.shape[-1]), filt_ld=row(hy_log_decay),
        hy_bias=hy_bias[0].reshape(1, HYENA_ORDER * D_HYENA).astype(F32),
    )
    f32 = lambda a: a[0].astype(F32)
    p.update(_s5_tables(f32(ssm_lam_re), f32(ssm_lam_im), f32(ssm_log_step), f32(ssm_b_re), f32(ssm_b_im),
                        f32(ssm_c_re), f32(ssm_c_im), f32(ssm_d)))
    return p


def _trunk(x, p):
    b, l, d = x.shape
    t = b * l
    per = D_HYENA // LANE
    h1, u, hy_raw = _ffn1_call(x.reshape(t, d), p)
    hy = _shortconv_call(hy_raw.reshape(b, l, D_HY_IN), p)
    y_ssm = _s5_call(u.reshape(b, l, D_SSM), p)
    tb = _dft_tables(l)
    kr, ki = _filter_call(l, p, tb)
    z1 = _conv_call(hy, 0, hy, per, p["hy_bias"], kr, ki, 0, tb)
    z2 = _conv_call(z1, 0, hy, 2 * per, p["hy_bias"], kr, ki, 1, tb)
    out = _final_call(y_ssm.reshape(t, D_SSM), z2.reshape(t, D_HYENA), h1, p)
    return out.reshape(b, l, d)


def kernel(x_prompt, x_sample, ffn1_w_gate, ffn1_w_up, ffn1_w_down, ln1_g, ln1_b, w_in, ssm_lam_re, ssm_lam_im, ssm_log_step, ssm_b_re, ssm_b_im, ssm_c_re, ssm_c_im, ssm_d, ssm_glu_w, ssm_glu_b, ssm_norm_g, hy_short_w, hy_short_b, hy_filt_w1, hy_filt_b1, hy_filt_w2, hy_filt_b2, hy_filt_w3, hy_sin_freq, hy_log_decay, hy_bias, hy_norm_g, w_out, ln2_g, ln2_b, ffn2_w_gate, ffn2_w_up, ffn2_w_down, ln3_g, ln3_b):
    p = _prepare(ffn1_w_gate, ffn1_w_up, ffn1_w_down, ln1_g, ln1_b, w_in,
                 ssm_lam_re, ssm_lam_im, ssm_log_step, ssm_b_re, ssm_b_im, ssm_c_re, ssm_c_im,
                 ssm_d, ssm_glu_w, ssm_glu_b, ssm_norm_g,
                 hy_short_w, hy_short_b, hy_filt_w1, hy_filt_b1, hy_filt_w2, hy_filt_b2, hy_filt_w3,
                 hy_sin_freq, hy_log_decay, hy_bias, hy_norm_g, w_out, ln2_g, ln2_b,
                 ffn2_w_gate, ffn2_w_up, ffn2_w_down, ln3_g, ln3_b)
    return (_trunk(x_prompt, p), _trunk(x_sample, p))
```

```python
import functools
import math

import jax
import jax.numpy as jnp
from jax import lax
from jax.experimental import pallas as pl
from jax.experimental.pallas import tpu as pltpu

F32 = jnp.float32
BF16 = jnp.bfloat16

D_MODEL = 1024
D_SSM = 512
SSM_GROUP = 16
N_SSM_GROUPS = D_SSM // SSM_GROUP
SSM_STATE = 64
D_HYENA = 512
HYENA_ORDER = 2
HYENA_BANDS = 8
HYENA_POS_DIM = 1 + 2 * HYENA_BANDS
HYENA_FILTER_HIDDEN = 64
HYENA_TIME_SCALE = 4096.0
HYENA_MAX_PERIOD = 10000.0
D_HY_IN = (HYENA_ORDER + 1) * D_HYENA
D_IN = D_SSM + D_HY_IN
D_FF = 128 * math.ceil(8 * D_MODEL / 3 / 128)
LN_EPS = 1e-5
RMS_EPS = 1e-6
FILTER_EPS = 1e-6
DEPTH = 1
DEEPNORM_ALPHA = (2.0 * DEPTH) ** 0.25

LANE = 128
SUBLANE = 8
V7X_VMEM_BYTES = 64 * 1024 * 1024
VMEM_LIMIT_BYTES = V7X_VMEM_BYTES - 8 * 1024 * 1024

TOKEN_TILE = 512
FF_CHUNK = 256
S5_CHUNK = 16
S5_LANES = S5_CHUNK * LANE
S5_STATE_LANES = (LANE // SSM_GROUP) * SSM_STATE
K1_BLOCK_MAX = 24


def _params(semantics):
    return pltpu.CompilerParams(dimension_semantics=semantics, vmem_limit_bytes=VMEM_LIMIT_BYTES)


def _resident(shape):
    nd = len(shape)
    return pl.BlockSpec(shape, lambda *_: (0,) * nd, pipeline_mode=pl.Buffered(1))


def _layer_norm(x, g, b):
    mu = jnp.mean(x, axis=-1, keepdims=True)
    xc = x - mu
    var = jnp.mean(xc * xc, axis=-1, keepdims=True)
    return xc * lax.rsqrt(var + LN_EPS) * g + b


def _rms_norm(x, g):
    ms = jnp.mean(x * x, axis=-1, keepdims=True)
    return x * lax.rsqrt(ms + RMS_EPS) * g


def _ffn_ln(x, wg_ref, wu_ref, wd_ref, g_ref, b_ref, acc_ref):
    xb = x.astype(BF16)
    acc_ref[...] = jnp.zeros_like(acc_ref)

    def body(c, carry):
        off = pl.multiple_of(c * FF_CHUNK, FF_CHUNK)
        gate = jnp.dot(xb, wg_ref[:, pl.ds(off, FF_CHUNK)], preferred_element_type=F32)
        up = jnp.dot(xb, wu_ref[:, pl.ds(off, FF_CHUNK)], preferred_element_type=F32)
        act = (jax.nn.silu(gate) * up).astype(BF16)
        acc_ref[...] += jnp.dot(act, wd_ref[pl.ds(off, FF_CHUNK), :], preferred_element_type=F32)
        return carry

    lax.fori_loop(0, D_FF // FF_CHUNK, body, 0, unroll=True)
    return _layer_norm(DEEPNORM_ALPHA * x + 0.5 * acc_ref[...], g_ref[...], b_ref[...])


def _ffn1_kernel(x_ref, wg_ref, wu_ref, wd_ref, g_ref, b_ref, win_ref, h_ref, u_ref, hy_ref, acc_ref):
    h = _ffn_ln(x_ref[...], wg_ref, wu_ref, wd_ref, g_ref, b_ref, acc_ref)
    h_ref[...] = h
    hb = h.astype(BF16)
    u_ref[...] = jnp.dot(hb, win_ref[:, :D_SSM], preferred_element_type=F32)
    hy_ref[...] = jnp.dot(hb, win_ref[:, D_SSM:], preferred_element_type=F32)


def _ffn1_call(x2d, p):
    t = x2d.shape[0]
    tm = TOKEN_TILE
    row = lambda w: pl.BlockSpec((tm, w), lambda i: (i, 0))
    return pl.pallas_call(
        _ffn1_kernel,
        out_shape=(jax.ShapeDtypeStruct((t, D_MODEL), F32),
                   jax.ShapeDtypeStruct((t, D_SSM), F32),
                   jax.ShapeDtypeStruct((t, D_HY_IN), F32)),
        grid=(t // tm,),
        in_specs=[row(D_MODEL), _resident((D_MODEL, D_FF)), _resident((D_MODEL, D_FF)),
                  _resident((D_FF, D_MODEL)), _resident((1, D_MODEL)), _resident((1, D_MODEL)),
                  _resident((D_MODEL, D_IN))],
        out_specs=(row(D_MODEL), row(D_SSM), row(D_HY_IN)),
        scratch_shapes=[pltpu.VMEM((tm, D_MODEL), F32)],
        compiler_params=_params(("parallel",)),
        name="ffn1_ln1_proj",
    )(x2d, p["ffn1_wg"], p["ffn1_wu"], p["ffn1_wd"], p["ln1_g"], p["ln1_b"], p["w_in"])


def _final_kernel(ys_ref, yh_ref, h_ref, gw_ref, gb_ref, sg_ref, hg_ref, wo_ref, l2g_ref, l2b_ref,
                  wg_ref, wu_ref, wd_ref, l3g_ref, l3b_ref, o_ref, acc_ref, *, n2):
    per = TOKEN_TILE // n2

    def sub_tile(hh, carry):
        rows = pl.ds(pl.multiple_of(hh * TOKEN_TILE, TOKEN_TILE), TOKEN_TILE)
        y_hy = jnp.concatenate(
            [jnp.concatenate([yh_ref[ct, pl.ds(hh * per + jj, n2, stride=SUBLANE), :]
                              for ct in range(D_HYENA // LANE)], axis=1) for jj in range(per)], axis=0)
        g = jax.nn.gelu(ys_ref[rows, :])
        gate = jax.nn.sigmoid(jnp.dot(g.astype(BF16), gw_ref[...], preferred_element_type=F32) + gb_ref[...])
        y_ssm = _rms_norm(g * gate, sg_ref[...])
        y_hy = _rms_norm(y_hy, hg_ref[...])
        mix = (jnp.dot(y_ssm.astype(BF16), wo_ref[:D_SSM, :], preferred_element_type=F32)
               + jnp.dot(y_hy.astype(BF16), wo_ref[D_SSM:, :], preferred_element_type=F32))
        x2 = _layer_norm(DEEPNORM_ALPHA * h_ref[rows, :] + mix, l2g_ref[...], l2b_ref[...])
        o_ref[rows, :] = _ffn_ln(x2, wg_ref, wu_ref, wd_ref, l3g_ref, l3b_ref, acc_ref)
        return carry

    lax.fori_loop(0, SUBLANE * n2 // TOKEN_TILE, sub_tile, 0)


def _final_call(y_ssm, y_hy, h1, p):
    b, l, _ = y_ssm.shape
    tt = y_hy.shape[3]
    n2 = tt // SUBLANE
    assert tt % TOKEN_TILE == 0 and TOKEN_TILE % n2 == 0
    row = lambda w: pl.BlockSpec((None, tt, w), lambda i, a: (i, a, 0))
    return pl.pallas_call(
        functools.partial(_final_kernel, n2=n2),
        out_shape=jax.ShapeDtypeStruct((b, l, D_MODEL), F32),
        grid=(b, l // tt),
        in_specs=[row(D_SSM),
                  pl.BlockSpec((None, D_HYENA // LANE, None, tt, LANE), lambda i, a: (i, 0, a, 0, 0)),
                  row(D_MODEL),
                  _resident((D_SSM, D_SSM)), _resident((1, D_SSM)), _resident((1, D_SSM)),
                  _resident((1, D_HYENA)), _resident((D_MODEL, D_MODEL)),
                  _resident((1, D_MODEL)), _resident((1, D_MODEL)),
                  _resident((D_MODEL, D_FF)), _resident((D_MODEL, D_FF)), _resident((D_FF, D_MODEL)),
                  _resident((1, D_MODEL)), _resident((1, D_MODEL))],
        out_specs=row(D_MODEL),
        scratch_shapes=[pltpu.VMEM((TOKEN_TILE, D_MODEL), F32)],
        compiler_params=_params(("parallel", "parallel")),
        name="mix_ln2_ffn2_ln3",
    )(y_ssm, y_hy, h1, p["glu_w"], p["glu_b"], p["ssm_norm_g"], p["hy_norm_g"], p["w_out"],
      p["ln2_g"], p["ln2_b"], p["ffn2_wg"], p["ffn2_wu"], p["ffn2_wd"], p["ln3_g"], p["ln3_b"])


def _shortconv_kernel(x_ref, prev_ref, next_ref, w_ref, b_ref, o_ref, *, n2):
    j = pl.program_id(1)
    x = x_ref[...]
    rows = x.shape[0]
    row = lax.broadcasted_iota(jnp.int32, x.shape, 0)
    before = jnp.where(j > 0, prev_ref[SUBLANE - 1:SUBLANE, :], 0.0)
    after = jnp.where(j < pl.num_programs(1) - 1, next_ref[0:1, :], 0.0)
    x_m1 = jnp.where(row == 0, before, pltpu.roll(x, 1, 0))
    x_p1 = jnp.where(row == rows - 1, after, pltpu.roll(x, rows - 1, 0))
    y = ((b_ref[...] + x_m1 * w_ref[0:1, :]) + x * w_ref[1:2, :]) + x_p1 * w_ref[2:3, :]
    for ct in range(y.shape[1] // LANE):
        for t1 in range(SUBLANE):
            o_ref[ct, pl.ds(t1, n2, stride=SUBLANE), :] = y[t1 * n2:(t1 + 1) * n2, ct * LANE:(ct + 1) * LANE]


def _shortconv_call(hy_raw, p):
    b, l, c = hy_raw.shape
    n1, n2, _ = _fft_dims(l)
    tt = SUBLANE * n2
    per = tt // SUBLANE
    last = l // SUBLANE - 1
    return pl.pallas_call(
        functools.partial(_shortconv_kernel, n2=n2),
        out_shape=jax.ShapeDtypeStruct((b, c // LANE, l // tt, tt, LANE), F32),
        grid=(b, l // tt),
        in_specs=[pl.BlockSpec((None, tt, c), lambda i, j: (i, j, 0)),
                  pl.BlockSpec((None, SUBLANE, c), lambda i, j: (i, jnp.maximum(j * per - 1, 0), 0)),
                  pl.BlockSpec((None, SUBLANE, c), lambda i, j: (i, jnp.minimum((j + 1) * per, last), 0)),
                  _resident((3, c)), _resident((1, c))],
        out_specs=pl.BlockSpec((None, c // LANE, None, tt, LANE), lambda i, j: (i, 0, j, 0, 0)),
        compiler_params=_params(("parallel", "parallel")),
        name="hyena_shortconv",
    )(hy_raw, hy_raw, hy_raw, p["short_w"], p["short_b"])


def _s5_kernel(u_ref, m_ref, ein_ref, gout_ref, lam_ref, d_ref, y_ref, ucat_ref, st_ref, *, nch, rb):
    sl = S5_STATE_LANES
    for t in range(S5_CHUNK):
        ucat_ref[:, t * LANE:(t + 1) * LANE] = u_ref[pl.ds(t, nch, stride=S5_CHUNK), :].astype(BF16)

    def state_in(i, carry):
        r0 = pl.multiple_of(i * rb, rb)
        lhs = ucat_ref[pl.ds(r0, rb), :]
        for cb in range(4):
            st_ref[pl.ds(r0, rb), cb * sl:(cb + 1) * sl] = jnp.dot(
                lhs, ein_ref[:, cb * sl:(cb + 1) * sl], preferred_element_type=F32)
        return carry

    lax.fori_loop(0, nch // rb, state_in, 0)

    lfr, lfi = lam_ref[0:1, :], lam_ref[1:2, :]
    lbr, lbi = lam_ref[2:3, :], lam_ref[3:4, :]

    def scan(i, carry):
        sfr, sfi, sbr, sbi = carry
        j = nch - 1 - i
        xfr = st_ref[pl.ds(i, 1), 0:sl]
        xfi = st_ref[pl.ds(i, 1), sl:2 * sl]
        st_ref[pl.ds(i, 1), 0:sl] = sfr
        st_ref[pl.ds(i, 1), sl:2 * sl] = sfi
        xbr = st_ref[pl.ds(j, 1), 2 * sl:3 * sl]
        xbi = st_ref[pl.ds(j, 1), 3 * sl:4 * sl]
        st_ref[pl.ds(j, 1), 2 * sl:3 * sl] = sbr
        st_ref[pl.ds(j, 1), 3 * sl:4 * sl] = sbi
        return (lfr * sfr - lfi * sfi + xfr, lfr * sfi + lfi * sfr + xfi,
                lbr * sbr - lbi * sbi + xbr, lbr * sbi + lbi * sbr + xbi)

    zero = jnp.zeros((1, sl), F32)
    lax.fori_loop(0, nch, scan, (zero, zero, zero, zero))

    def emit(i, carry):
        r0 = pl.multiple_of(i * rb, rb)
        lhs_u = ucat_ref[pl.ds(r0, rb), :]
        lhs_s = st_ref[pl.ds(r0, rb), :].astype(BF16)
        for cb in range(4):
            cols = slice(cb * 4 * LANE, (cb + 1) * 4 * LANE)
            yc = (jnp.dot(lhs_u, m_ref[:, cols], preferred_element_type=F32)
                  + jnp.dot(lhs_s, gout_ref[:, cols], preferred_element_type=F32))
            for tt in range(4):
                rows = pl.ds(r0 * S5_CHUNK + cb * 4 + tt, rb, stride=S5_CHUNK)
                y_ref[rows, :] = yc[:, tt * LANE:(tt + 1) * LANE] + d_ref[...] * u_ref[rows, :]
        return carry

    lax.fori_loop(0, nch // rb, emit, 0)


def _s5_call(u, p):
    b, l, _ = u.shape
    nch = l // S5_CHUNK
    rb = min(nch, 256)
    nq = D_SSM // LANE
    mat = pl.BlockSpec((None, S5_LANES, S5_LANES), lambda q, i: (q, 0, 0), pipeline_mode=pl.Buffered(1))
    return pl.pallas_call(
        functools.partial(_s5_kernel, nch=nch, rb=rb),
        out_shape=jax.ShapeDtypeStruct((b, l, D_SSM), F32),
        grid=(nq, b),
        in_specs=[pl.BlockSpec((None, l, LANE), lambda q, i: (i, 0, q)),
                  mat, mat, mat,
                  pl.BlockSpec((None, 4, S5_STATE_LANES), lambda q, i: (q, 0, 0)),
                  pl.BlockSpec((None, 1, LANE), lambda q, i: (q, 0, 0))],
        out_specs=pl.BlockSpec((None, l, LANE), lambda q, i: (i, 0, q)),
        scratch_shapes=[pltpu.VMEM((nch, S5_LANES), BF16), pltpu.VMEM((nch, 4 * S5_STATE_LANES), F32)],
        compiler_params=_params(("parallel", "parallel")),
        name="s5_chunked",
    )(u, p["s5_m"], p["s5_ein"], p["s5_gout"], p["s5_lam"], p["s5_d"])


def _cmul(ar, ai, br, bi):
    return ar * br - ai * bi, ar * bi + ai * br


def _s5_tables(lam_re, lam_im, log_step, b_re, b_im, c_re, c_im, d):
    g, hh, pp, tc = N_SSM_GROUPS, SSM_GROUP, SSM_STATE, S5_CHUNK
    gl = LANE // hh
    nq = g // gl
    step = jnp.exp(log_step)[..., None]
    mag = jnp.exp(lam_re * step)
    ar = mag * jnp.cos(lam_im * step)
    ai = mag * jnp.sin(lam_im * step)
    nr, ni = ar - 1.0, ai
    den = lam_re * lam_re + lam_im * lam_im
    qr = (nr * lam_re + ni * lam_im) / den
    qi = (ni * lam_re - nr * lam_im) / den
    bbr = qr[..., None] * b_re - qi[..., None] * b_im
    bbi = qr[..., None] * b_im + qi[..., None] * b_re
    pr, pi = [jnp.ones_like(ar)], [jnp.zeros_like(ai)]
    for _ in range(tc):
        r, i = _cmul(pr[-1], pi[-1], ar, ai)
        pr.append(r)
        pi.append(i)
    pwr, pwi = jnp.stack(pr), jnp.stack(pi)
    eye = jnp.eye(gl, dtype=F32)
    exact = lax.Precision.HIGHEST

    def c_times_pow(dd, sel):
        return _cmul(c_re[dd][None], c_im[dd][None], pwr[sel, dd][:, :, None, :], pwi[sel, dd][:, :, None, :])

    def lag_kernels(dd):
        zr, zi = c_times_pow(dd, jnp.arange(tc))
        return (jnp.einsum("jgap,gpb->jgab", zr, bbr[dd], precision=exact)
                - jnp.einsum("jgap,gpb->jgab", zi, bbi[dd], precision=exact))

    kf, kb = lag_kernels(0), lag_kernels(1)
    kall = jnp.concatenate([kb[jnp.arange(tc - 1, 0, -1)], kf[:1] + kb[:1], kf[1:]], axis=0)
    k6 = kall.reshape(2 * tc - 1, nq, gl, hh, hh).transpose(0, 1, 2, 4, 3)
    blk = (k6[:, :, :, :, None, :] * eye[:, None, :, None]).reshape(2 * tc - 1, nq, LANE, LANE).astype(BF16)
    tok = jnp.arange(tc)
    m = blk[tok[None, :] - tok[:, None] + tc - 1]
    m = m.transpose(2, 0, 3, 1, 4).reshape(nq, S5_LANES, S5_LANES)

    eye16 = eye.astype(BF16)

    def tile_in(coef):
        c6 = coef.astype(BF16).reshape(tc, nq, gl, pp, hh).transpose(1, 0, 2, 4, 3)
        return (c6[:, :, :, :, None, :] * eye16[:, None, :, None]).reshape(nq, S5_LANES, gl * pp)

    def state_in(dd, sel):
        return _cmul(pwr[sel, dd][..., None], pwi[sel, dd][..., None], bbr[dd][None], bbi[dd][None])

    ein = jnp.concatenate([tile_in(c) for c in state_in(0, jnp.arange(tc - 1, -1, -1))]
                          + [tile_in(c) for c in state_in(1, jnp.arange(tc))], axis=2)

    def tile_out(coef):
        c6 = coef.astype(BF16).reshape(tc, nq, gl, hh, pp).transpose(1, 2, 4, 0, 3)
        return (c6[:, :, :, :, None, :] * eye16[:, None, None, :, None]).reshape(nq, gl * pp, S5_LANES)

    ofr, ofi = c_times_pow(0, jnp.arange(1, tc + 1))
    obr, obi = c_times_pow(1, jnp.arange(tc, 0, -1))
    gout = jnp.concatenate([tile_out(ofr), tile_out(-ofi), tile_out(obr), tile_out(-obi)], axis=1)

    lam16 = jnp.stack([pwr[tc, 0], pwi[tc, 0], pwr[tc, 1], pwi[tc, 1]], axis=0)
    lam16 = lam16.reshape(4, nq, gl * pp).transpose(1, 0, 2)
    return dict(s5_m=m, s5_ein=ein, s5_gout=gout,
                s5_lam=lam16.astype(F32), s5_d=d.reshape(nq, 1, LANE).astype(F32))


def _fft_dims(l):
    n = 2 * l
    bits = n.bit_length() - 1
    assert n == 1 << bits and bits % 2 == 0, "sequence length must give a square power-of-two DFT size"
    n1 = 1 << (bits // 2)
    hp = -(-(n1 // 2 + 1) // SUBLANE) * SUBLANE
    return n1, n1, hp


def _t2_rows(t2):
    return pl.ds(pl.multiple_of(t2 * SUBLANE, SUBLANE), SUBLANE)


def _a_rows(t2hi, t2lo, hp):
    return pl.ds(t2hi * (hp * SUBLANE) + t2lo, hp, stride=SUBLANE)


def _a_tile(t2hi, k1, hp):
    return pl.ds(pl.multiple_of((t2hi * hp + k1) * SUBLANE, SUBLANE), SUBLANE)


def _a_load_k1(ref, k1, n2, hp):
    return jnp.concatenate([ref[_a_tile(t, k1, hp), :] for t in range(n2 // SUBLANE)], axis=0)


def _a_store_k1(ref, k1, val, n2, hp):
    for t in range(n2 // SUBLANE):
        ref[_a_tile(t, k1, hp), :] = val[t * SUBLANE:(t + 1) * SUBLANE, :]


def _k1_block(hp):
    return max(d for d in range(2, K1_BLOCK_MAX + 1, 2) if hp % d == 0)


def _dft_tables(l):
    n = 2 * l
    n1, n2, hp = _fft_dims(l)
    t2 = jnp.arange(n2, dtype=jnp.int32)[:, None, None]
    k1 = jnp.arange(hp, dtype=jnp.int32)[None, :, None]
    t1 = jnp.arange(n1, dtype=jnp.int32)[None, None, :]
    ang = (2.0 * math.pi / n) * (((t1 * n2 + t2) * k1) % n).astype(F32)
    cos1, sin1 = jnp.cos(ang), jnp.sin(ang)
    e1 = jnp.concatenate([cos1, -sin1], axis=1)
    wgt = jnp.where((k1 == 0) | (k1 == n1 // 2), 1.0, jnp.where(k1 < n1 // 2, 2.0, 0.0))
    half = n1 // 2
    einv = jnp.concatenate([(wgt * cos1)[:, :, :half], (-wgt * sin1)[:, :, :half]], axis=1)
    einv = einv.transpose(0, 2, 1)
    a = jnp.arange(n2, dtype=jnp.int32)
    ang2 = (2.0 * math.pi / n2) * ((a[:, None] * a[None, :]) % n2).astype(F32)
    c2, s2 = jnp.cos(ang2), jnp.sin(ang2)
    f2c = jnp.block([[c2, s2], [-s2, c2]])
    f2ic = jnp.block([[c2, -s2], [s2, c2]])
    return dict(e1=e1.astype(BF16), einv=einv.astype(BF16), f2c=f2c.astype(BF16), f2ic=f2ic.astype(BF16))


def _filter_kernel(w1_ref, b1_ref, w2_ref, b2_ref, sf_ref, om_ref, w3f_ref, w3b_ref, ldf_ref, ldb_ref,
                   e1_ref, f2c_ref, kr_ref, ki_ref, hdn_ref, are_ref, aim_ref, scale_ref, *, l, n1, n2, hp, k1b):
    j = pl.program_id(0)
    kb = pl.program_id(1)
    n = 2 * l

    def positions(t2):
        t1 = lax.broadcasted_iota(jnp.int32, (n1, LANE), 0)
        i = t1 * n2 + t2
        return i, jnp.where(i <= l, i, n - i).astype(F32)

    @pl.when((j == 0) & (kb == 0))
    def _():
        def body(t2, carry):
            _, pos = positions(t2)
            ang = pos * om_ref[...]
            lane = lax.broadcasted_iota(jnp.int32, (n1, LANE), 1)
            feats = jnp.where(lane == 0, pos / HYENA_TIME_SCALE,
                              jnp.where(lane <= HYENA_BANDS, jnp.sin(ang),
                                        jnp.where(lane <= 2 * HYENA_BANDS, jnp.cos(ang), 0.0)))
            h1 = jnp.sin(sf_ref[0:1, :] * (jnp.dot(feats, w1_ref[...], preferred_element_type=F32) + b1_ref[...]))
            h2 = jnp.sin(sf_ref[1:2, :] * (jnp.dot(h1, w2_ref[...], preferred_element_type=F32) + b2_ref[...]))
            hdn_ref[pl.ds(pl.multiple_of(t2 * n1, n1), n1), :] = h2
            return carry
        lax.fori_loop(0, n2, body, 0)

    @pl.when(kb == 0)
    def _():
        def body(t2hi, asum):
            for t2lo in range(SUBLANE):
                t2 = t2hi * SUBLANE + t2lo
                i, pos = positions(t2)
                h2 = hdn_ref[pl.ds(pl.multiple_of(t2 * n1, n1), n1), :]
                t_lin = pos / HYENA_TIME_SCALE
                fwd = jnp.dot(h2, w3f_ref[...], preferred_element_type=F32) * jnp.exp(-t_lin * jnp.exp(ldf_ref[...]))
                bwd = jnp.dot(h2, w3b_ref[...], preferred_element_type=F32) * jnp.exp(-t_lin * jnp.exp(ldb_ref[...]))
                k = jnp.where(i < l, fwd, jnp.where(i > l, bwd, 0.0))
                r = jnp.dot(e1_ref[t2], k.astype(BF16), preferred_element_type=F32)
                are_ref[_a_rows(t2hi, t2lo, hp), :] = r[:hp]
                aim_ref[_a_rows(t2hi, t2lo, hp), :] = r[hp:]
                asum = asum + jnp.sum(jnp.abs(k), axis=0, keepdims=True)
            return asum
        asum = lax.fori_loop(0, n2 // SUBLANE, body, jnp.zeros((1, LANE), F32))
        scale_ref[...] = 1.0 / ((asum + FILTER_EPS) * n)

    for kk in range(0, k1b, 2):
        k1 = kb * k1b + kk
        pair = lambda ref: jnp.concatenate([_a_load_k1(ref, k1 + e, n2, hp) for e in range(2)], axis=1)
        rhs = jnp.concatenate([pair(are_ref), pair(aim_ref)], axis=0).astype(BF16)
        x = jnp.dot(f2c_ref[...], rhs, preferred_element_type=F32)
        for e in range(2):
            rows = slice((kk + e) * n2, (kk + e + 1) * n2)
            kr_ref[rows, :] = x[:n2, e * LANE:(e + 1) * LANE] * scale_ref[...]
            ki_ref[rows, :] = x[n2:, e * LANE:(e + 1) * LANE] * scale_ref[...]


def _filter_call(l, p, tb):
    n1, n2, hp = _fft_dims(l)
    k1b = _k1_block(hp)
    n = 2 * l
    nt = HYENA_ORDER * D_HYENA // LANE
    per = D_HYENA // LANE
    fwd_col = lambda j, kb: (0, (j // per) * 2 * per + j % per)
    bwd_col = lambda j, kb: (0, (j // per) * 2 * per + per + j % per)
    spec_out = pl.BlockSpec((k1b * n2, LANE), lambda j, kb: (kb, j))
    a_shape = (n2 * hp, LANE)
    return pl.pallas_call(
        functools.partial(_filter_kernel, l=l, n1=n1, n2=n2, hp=hp, k1b=k1b),
        out_shape=(jax.ShapeDtypeStruct((hp * n2, nt * LANE), F32),) * 2,
        grid=(nt, hp // k1b),
        in_specs=[_resident((LANE, LANE)), _resident((1, LANE)), _resident((LANE, LANE)), _resident((1, LANE)),
                  _resident((2, LANE)), _resident((1, LANE)),
                  pl.BlockSpec((LANE, LANE), fwd_col), pl.BlockSpec((LANE, LANE), bwd_col),
                  pl.BlockSpec((1, LANE), fwd_col), pl.BlockSpec((1, LANE), bwd_col),
                  _resident((n2, 2 * hp, n1)), _resident((2 * n2, 2 * n2))],
        out_specs=(spec_out, spec_out),
        scratch_shapes=[pltpu.VMEM((n, LANE), F32), pltpu.VMEM(a_shape, F32),
                        pltpu.VMEM(a_shape, F32), pltpu.VMEM((1, LANE), F32)],
        compiler_params=_params(("arbitrary", "arbitrary")),
        name="hyena_filter_spectrum",
    )(p["filt_w1"], p["filt_b1"], p["filt_w2"], p["filt_b2"], p["filt_sf"], p["filt_om"],
      p["filt_w3"], p["filt_w3"], p["filt_ld"], p["filt_ld"], tb["e1"], tb["f2c"])


def _conv_kernel(z_ref, g_ref, bias_ref, e1_ref, einv_ref, f2c_ref, f2ic_ref, kr_ref, ki_ref, o_ref,
                 are_ref, aim_ref, *, n1, n2, hp, k1b):
    s = pl.program_id(2)
    half = n1 // 2

    @pl.when(s == 0)
    def _():
        def body(t2hi, carry):
            for t2lo in range(SUBLANE):
                t2 = t2hi * SUBLANE + t2lo
                z = z_ref[:, _t2_rows(t2), :].reshape(half, LANE).astype(BF16)
                r = jnp.dot(e1_ref[t2, :, 0:half], z, preferred_element_type=F32)
                are_ref[_a_rows(t2hi, t2lo, hp), :] = r[:hp]
                aim_ref[_a_rows(t2hi, t2lo, hp), :] = r[hp:]
            return carry
        lax.fori_loop(0, n2 // SUBLANE, body, 0)

    for kk in range(0, k1b, 2):
        k1 = s * k1b + kk
        pair = lambda ref: jnp.concatenate([_a_load_k1(ref, k1 + e, n2, hp) for e in range(2)], axis=1)
        rhs = jnp.concatenate([pair(are_ref), pair(aim_ref)], axis=0).astype(BF16)
        x = jnp.dot(f2c_ref[...], rhs, preferred_element_type=F32)
        xr, xi = x[:n2], x[n2:]
        spec = lambda ref: jnp.concatenate([ref[(kk + e) * n2:(kk + e + 1) * n2, :] for e in range(2)], axis=1)
        fr, fi = spec(kr_ref), spec(ki_ref)
        y = jnp.concatenate([xr * fr - xi * fi, xr * fi + xi * fr], axis=0).astype(BF16)
        back = jnp.dot(f2ic_ref[...], y, preferred_element_type=F32)
        for e in range(2):
            _a_store_k1(are_ref, k1 + e, back[:n2, e * LANE:(e + 1) * LANE], n2, hp)
            _a_store_k1(aim_ref, k1 + e, back[n2:, e * LANE:(e + 1) * LANE], n2, hp)

    @pl.when(s == pl.num_programs(2) - 1)
    def _():
        def body(t2hi, carry):
            for t2lo in range(SUBLANE):
                t2 = t2hi * SUBLANE + t2lo
                rows = _a_rows(t2hi, t2lo, hp)
                rhs = jnp.concatenate([are_ref[rows, :], aim_ref[rows, :]], axis=0).astype(BF16)
                y = jnp.dot(einv_ref[t2], rhs, preferred_element_type=F32)
                t1 = (half // SUBLANE, SUBLANE, LANE)
                out = g_ref[:, _t2_rows(t2), :] * (y.reshape(t1) + bias_ref[...] * z_ref[:, _t2_rows(t2), :])
                o_ref[:, _t2_rows(t2), :] = out
            return carry
        lax.fori_loop(0, n2 // SUBLANE, body, 0)


def _conv_call(z, z_col0, g, g_col0, bias, kr, ki, order, tb):
    b, _, t1_tiles, tt, _ = z.shape
    n2, half = tt // SUBLANE, t1_tiles * SUBLANE
    n1, _, hp = _fft_dims(n2 * half)
    k1b = _k1_block(hp)
    per = D_HYENA // LANE
    seq = lambda col0: pl.BlockSpec((None, None, t1_tiles, tt, LANE), lambda c, i, s: (i, col0 + c, 0, 0, 0))
    spec_k = pl.BlockSpec((k1b * n2, LANE), lambda c, i, s: (s, order * per + c))
    a_shape = (n2 * hp, LANE)
    return pl.pallas_call(
        functools.partial(_conv_kernel, n1=n1, n2=n2, hp=hp, k1b=k1b),
        out_shape=jax.ShapeDtypeStruct((b, per, t1_tiles, tt, LANE), F32),
        grid=(per, b, hp // k1b),
        in_specs=[seq(z_col0), seq(g_col0),
                  pl.BlockSpec((1, LANE), lambda c, i, s: (0, order * per + c)),
                  _resident((n2, 2 * hp, n1)), _resident((n2, n1 // 2, 2 * hp)),
                  _resident((2 * n2, 2 * n2)), _resident((2 * n2, 2 * n2)),
                  spec_k, spec_k],
        out_specs=pl.BlockSpec((None, None, t1_tiles, tt, LANE), lambda c, i, s: (i, c, 0, 0, 0)),
        scratch_shapes=[pltpu.VMEM(a_shape, F32), pltpu.VMEM(a_shape, F32)],
        compiler_params=_params(("parallel", "parallel", "arbitrary")),
        name=f"hyena_conv_order{order}",
    )(z, g, bias, tb["e1"], tb["einv"], tb["f2c"], tb["f2ic"], kr, ki)


def _prepare(ffn1_w_gate, ffn1_w_up, ffn1_w_down, ln1_g, ln1_b, w_in,
             ssm_lam_re, ssm_lam_im, ssm_log_step, ssm_b_re, ssm_b_im, ssm_c_re, ssm_c_im,
             ssm_d, ssm_glu_w, ssm_glu_b, ssm_norm_g,
             hy_short_w, hy_short_b, hy_filt_w1, hy_filt_b1, hy_filt_w2, hy_filt_b2, hy_filt_w3,
             hy_sin_freq, hy_log_decay, hy_bias, hy_norm_g, w_out, ln2_g, ln2_b,
             ffn2_w_gate, ffn2_w_up, ffn2_w_down, ln3_g, ln3_b):
    row = lambda a: a[0].reshape(1, -1).astype(F32)
    hid = HYENA_FILTER_HIDDEN
    pad2 = lambda a, r, c: jnp.zeros((r, c), F32).at[:a.shape[0], :a.shape[1]].set(a.astype(F32))
    omega = jnp.exp(-math.log(HYENA_MAX_PERIOD) * jnp.arange(HYENA_BANDS, dtype=F32) / HYENA_BANDS)
    om = jnp.zeros((1, LANE), F32).at[0, 1:1 + HYENA_BANDS].set(omega).at[0, 1 + HYENA_BANDS:HYENA_POS_DIM].set(omega)
    p = dict(
        ffn1_wg=ffn1_w_gate[0].astype(BF16), ffn1_wu=ffn1_w_up[0].astype(BF16), ffn1_wd=ffn1_w_down[0].astype(BF16),
        ln1_g=row(ln1_g), ln1_b=row(ln1_b), w_in=w_in[0].astype(BF16),
        glu_w=ssm_glu_w[0].astype(BF16), glu_b=row(ssm_glu_b), ssm_norm_g=row(ssm_norm_g),
        hy_norm_g=row(hy_norm_g), w_out=w_out[0].astype(BF16), ln2_g=row(ln2_g), ln2_b=row(ln2_b),
        ffn2_wg=ffn2_w_gate[0].astype(BF16), ffn2_wu=ffn2_w_up[0].astype(BF16), ffn2_wd=ffn2_w_down[0].astype(BF16),
        ln3_g=row(ln3_g), ln3_b=row(ln3_b),
        short_w=hy_short_w[0].astype(F32), short_b=row(hy_short_b),
        filt_w1=pad2(hy_filt_w1[0], LANE, LANE), filt_b1=pad2(hy_filt_b1[0].reshape(1, hid), 1, LANE),
        filt_w2=pad2(hy_filt_w2[0], LANE, LANE), filt_b2=pad2(hy_filt_b2[0].reshape(1, hid), 1, LANE),
        filt_sf=pad2(hy_sin_freq[0], 2, LANE), filt_om=om,
        filt_w3=pad2(hy_filt_w3[0], LANE, hy_filt_w3.shape[-1]), filt_ld=row(hy_log_decay),
        hy_bias=hy_bias[0].reshape(1, HYENA_ORDER * D_HYENA).astype(F32),
    )
    f32 = lambda a: a[0].astype(F32)
    p.update(_s5_tables(f32(ssm_lam_re), f32(ssm_lam_im), f32(ssm_log_step), f32(ssm_b_re), f32(ssm_b_im),
                        f32(ssm_c_re), f32(ssm_c_im), f32(ssm_d)))
    return p


def _trunk(x, p):
    b, l, d = x.shape
    t = b * l
    per = D_HYENA // LANE
    h1, u, hy_raw = _ffn1_call(x.reshape(t, d), p)
    hy = _shortconv_call(hy_raw.reshape(b, l, D_HY_IN), p)
    y_ssm = _s5_call(u.reshape(b, l, D_SSM), p)
    tb = _dft_tables(l)
    kr, ki = _filter_call(l, p, tb)
    z1 = _conv_call(hy, 0, hy, per, p["hy_bias"], kr, ki, 0, tb)
    z2 = _conv_call(z1, 0, hy, 2 * per, p["hy_bias"], kr, ki, 1, tb)
    return _final_call(y_ssm, z2, h1.reshape(b, l, d), p)


def kernel(x_prompt, x_sample, ffn1_w_gate, ffn1_w_up, ffn1_w_down, ln1_g, ln1_b, w_in, ssm_lam_re, ssm_lam_im, ssm_log_step, ssm_b_re, ssm_b_im, ssm_c_re, ssm_c_im, ssm_d, ssm_glu_w, ssm_glu_b, ssm_norm_g, hy_short_w, hy_short_b, hy_filt_w1, hy_filt_b1, hy_filt_w2, hy_filt_b2, hy_filt_w3, hy_sin_freq, hy_log_decay, hy_bias, hy_norm_g, w_out, ln2_g, ln2_b, ffn2_w_gate, ffn2_w_up, ffn2_w_down, ln3_g, ln3_b):
    p = _prepare(ffn1_w_gate, ffn1_w_up, ffn1_w_down, ln1_g, ln1_b, w_in,
                 ssm_lam_re, ssm_lam_im, ssm_log_step, ssm_b_re, ssm_b_im, ssm_c_re, ssm_c_im,
                 ssm_d, ssm_glu_w, ssm_glu_b, ssm_norm_g,
                 hy_short_w, hy_short_b, hy_filt_w1, hy_filt_b1, hy_filt_w2, hy_filt_b2, hy_filt_w3,
                 hy_sin_freq, hy_log_decay, hy_bias, hy_norm_g, w_out, ln2_g, ln2_b,
                 ffn2_w_gate, ffn2_w_up, ffn2_w_down, ln3_g, ln3_b)
    return (_trunk(x_prompt, p), _trunk(x_sample, p))
```

```python
import functools
import math

import jax
import jax.numpy as jnp
from jax import lax
from jax.experimental import pallas as pl
from jax.experimental.pallas import tpu as pltpu

F32 = jnp.float32
BF16 = jnp.bfloat16

D_MODEL = 1024
D_SSM = 512
SSM_GROUP = 16
N_SSM_GROUPS = D_SSM // SSM_GROUP
SSM_STATE = 64
D_HYENA = 512
HYENA_ORDER = 2
HYENA_BANDS = 8
HYENA_POS_DIM = 1 + 2 * HYENA_BANDS
HYENA_FILTER_HIDDEN = 64
HYENA_TIME_SCALE = 4096.0
HYENA_MAX_PERIOD = 10000.0
D_HY_IN = (HYENA_ORDER + 1) * D_HYENA
D_IN = D_SSM + D_HY_IN
D_FF = 128 * math.ceil(8 * D_MODEL / 3 / 128)
LN_EPS = 1e-5
RMS_EPS = 1e-6
FILTER_EPS = 1e-6
DEPTH = 1
DEEPNORM_ALPHA = (2.0 * DEPTH) ** 0.25

LANE = 128
SUBLANE = 8
V7X_VMEM_BYTES = 64 * 1024 * 1024
VMEM_LIMIT_BYTES = V7X_VMEM_BYTES - 8 * 1024 * 1024

TOKEN_TILE = 512
FF_CHUNK = 256
S5_CHUNK = 16
S5_LANES = S5_CHUNK * LANE
S5_STATE_LANES = (LANE // SSM_GROUP) * SSM_STATE
K1_BLOCK_MAX = 24


def _params(semantics):
    return pltpu.CompilerParams(dimension_semantics=semantics, vmem_limit_bytes=VMEM_LIMIT_BYTES)


def _resident(shape):
    nd = len(shape)
    return pl.BlockSpec(shape, lambda *_: (0,) * nd, pipeline_mode=pl.Buffered(1))


def _layer_norm(x, g, b):
    mu = jnp.mean(x, axis=-1, keepdims=True)
    xc = x - mu
    var = jnp.mean(xc * xc, axis=-1, keepdims=True)
    return xc * lax.rsqrt(var + LN_EPS) * g + b


def _rms_norm(x, g):
    ms = jnp.mean(x * x, axis=-1, keepdims=True)
    return x * lax.rsqrt(ms + RMS_EPS) * g


def _ffn_ln(x, wg_ref, wu_ref, wd_ref, g_ref, b_ref, acc_ref):
    xb = x.astype(BF16)
    acc_ref[...] = jnp.zeros_like(acc_ref)

    def body(c, carry):
        off = pl.multiple_of(c * FF_CHUNK, FF_CHUNK)
        gate = jnp.dot(xb, wg_ref[:, pl.ds(off, FF_CHUNK)], preferred_element_type=F32)
        up = jnp.dot(xb, wu_ref[:, pl.ds(off, FF_CHUNK)], preferred_element_type=F32)
        act = (jax.nn.silu(gate) * up).astype(BF16)
        acc_ref[...] += jnp.dot(act, wd_ref[pl.ds(off, FF_CHUNK), :], preferred_element_type=F32)
        return carry

    lax.fori_loop(0, D_FF // FF_CHUNK, body, 0, unroll=True)
    return _layer_norm(DEEPNORM_ALPHA * x + 0.5 * acc_ref[...], g_ref[...], b_ref[...])


def _ffn1_kernel(x_ref, wg_ref, wu_ref, wd_ref, g_ref, b_ref, win_ref, h_ref, u_ref, hy_ref, acc_ref):
    h = _ffn_ln(x_ref[...], wg_ref, wu_ref, wd_ref, g_ref, b_ref, acc_ref)
    h_ref[...] = h
    hb = h.astype(BF16)
    u_ref[...] = jnp.dot(hb, win_ref[:, :D_SSM], preferred_element_type=F32)
    hy_ref[...] = jnp.dot(hb, win_ref[:, D_SSM:], preferred_element_type=F32)


def _ffn1_call(x2d, p):
    t = x2d.shape[0]
    tm = TOKEN_TILE
    row = lambda w: pl.BlockSpec((tm, w), lambda i: (i, 0))
    return pl.pallas_call(
        _ffn1_kernel,
        out_shape=(jax.ShapeDtypeStruct((t, D_MODEL), F32),
                   jax.ShapeDtypeStruct((t, D_SSM), F32),
                   jax.ShapeDtypeStruct((t, D_HY_IN), F32)),
        grid=(t // tm,),
        in_specs=[row(D_MODEL), _resident((D_MODEL, D_FF)), _resident((D_MODEL, D_FF)),
                  _resident((D_FF, D_MODEL)), _resident((1, D_MODEL)), _resident((1, D_MODEL)),
                  _resident((D_MODEL, D_IN))],
        out_specs=(row(D_MODEL), row(D_SSM), row(D_HY_IN)),
        scratch_shapes=[pltpu.VMEM((tm, D_MODEL), F32)],
        compiler_params=_params(("parallel",)),
        name="ffn1_ln1_proj",
    )(x2d, p["ffn1_wg"], p["ffn1_wu"], p["ffn1_wd"], p["ln1_g"], p["ln1_b"], p["w_in"])


def _final_kernel(ys_ref, yh_ref, h_ref, gw_ref, gb_ref, sg_ref, hg_ref, wo_ref, l2g_ref, l2b_ref,
                  wg_ref, wu_ref, wd_ref, l3g_ref, l3b_ref, o_ref, acc_ref, *, n2):
    per = TOKEN_TILE // n2

    def sub_tile(hh, carry):
        rows = pl.ds(pl.multiple_of(hh * TOKEN_TILE, TOKEN_TILE), TOKEN_TILE)
        y_hy = jnp.concatenate(
            [jnp.concatenate([yh_ref[ct, pl.ds(hh * per + jj, n2, stride=SUBLANE), :]
                              for ct in range(D_HYENA // LANE)], axis=1) for jj in range(per)], axis=0)
        g = jax.nn.gelu(ys_ref[rows, :])
        gate = jax.nn.sigmoid(jnp.dot(g.astype(BF16), gw_ref[...], preferred_element_type=F32) + gb_ref[...])
        y_ssm = _rms_norm(g * gate, sg_ref[...])
        y_hy = _rms_norm(y_hy, hg_ref[...])
        mix = (jnp.dot(y_ssm.astype(BF16), wo_ref[:D_SSM, :], preferred_element_type=F32)
               + jnp.dot(y_hy.astype(BF16), wo_ref[D_SSM:, :], preferred_element_type=F32))
        x2 = _layer_norm(DEEPNORM_ALPHA * h_ref[rows, :] + mix, l2g_ref[...], l2b_ref[...])
        o_ref[rows, :] = _ffn_ln(x2, wg_ref, wu_ref, wd_ref, l3g_ref, l3b_ref, acc_ref)
        return carry

    lax.fori_loop(0, SUBLANE * n2 // TOKEN_TILE, sub_tile, 0)


def _final_call(y_ssm, y_hy, h1, p):
    b, l, _ = y_ssm.shape
    tt = y_hy.shape[3]
    n2 = tt // SUBLANE
    assert tt % TOKEN_TILE == 0 and TOKEN_TILE % n2 == 0
    row = lambda w: pl.BlockSpec((None, tt, w), lambda i, a: (i, a, 0))
    return pl.pallas_call(
        functools.partial(_final_kernel, n2=n2),
        out_shape=jax.ShapeDtypeStruct((b, l, D_MODEL), F32),
        grid=(b, l // tt),
        in_specs=[row(D_SSM),
                  pl.BlockSpec((None, D_HYENA // LANE, None, tt, LANE), lambda i, a: (i, 0, a, 0, 0)),
                  row(D_MODEL),
                  _resident((D_SSM, D_SSM)), _resident((1, D_SSM)), _resident((1, D_SSM)),
                  _resident((1, D_HYENA)), _resident((D_MODEL, D_MODEL)),
                  _resident((1, D_MODEL)), _resident((1, D_MODEL)),
                  _resident((D_MODEL, D_FF)), _resident((D_MODEL, D_FF)), _resident((D_FF, D_MODEL)),
                  _resident((1, D_MODEL)), _resident((1, D_MODEL))],
        out_specs=row(D_MODEL),
        scratch_shapes=[pltpu.VMEM((TOKEN_TILE, D_MODEL), F32)],
        compiler_params=_params(("parallel", "parallel")),
        name="mix_ln2_ffn2_ln3",
    )(y_ssm, y_hy, h1, p["glu_w"], p["glu_b"], p["ssm_norm_g"], p["hy_norm_g"], p["w_out"],
      p["ln2_g"], p["ln2_b"], p["ffn2_wg"], p["ffn2_wu"], p["ffn2_wd"], p["ln3_g"], p["ln3_b"])


def _shortconv_kernel(x_ref, prev_ref, next_ref, w_ref, b_ref, o_ref, *, n2):
    j = pl.program_id(1)
    x = x_ref[...]
    rows = x.shape[0]
    row = lax.broadcasted_iota(jnp.int32, x.shape, 0)
    before = jnp.where(j > 0, prev_ref[SUBLANE - 1:SUBLANE, :], 0.0)
    after = jnp.where(j < pl.num_programs(1) - 1, next_ref[0:1, :], 0.0)
    x_m1 = jnp.where(row == 0, before, pltpu.roll(x, 1, 0))
    x_p1 = jnp.where(row == rows - 1, after, pltpu.roll(x, rows - 1, 0))
    y = ((b_ref[...] + x_m1 * w_ref[0:1, :]) + x * w_ref[1:2, :]) + x_p1 * w_ref[2:3, :]
    for ct in range(y.shape[1] // LANE):
        for t1 in range(SUBLANE):
            o_ref[ct, pl.ds(t1, n2, stride=SUBLANE), :] = y[t1 * n2:(t1 + 1) * n2, ct * LANE:(ct + 1) * LANE]


def _shortconv_call(hy_raw, p):
    b, l, c = hy_raw.shape
    n1, n2, _ = _fft_dims(l)
    tt = SUBLANE * n2
    per = tt // SUBLANE
    last = l // SUBLANE - 1
    return pl.pallas_call(
        functools.partial(_shortconv_kernel, n2=n2),
        out_shape=jax.ShapeDtypeStruct((b, c // LANE, l // tt, tt, LANE), F32),
        grid=(b, l // tt),
        in_specs=[pl.BlockSpec((None, tt, c), lambda i, j: (i, j, 0)),
                  pl.BlockSpec((None, SUBLANE, c), lambda i, j: (i, jnp.maximum(j * per - 1, 0), 0)),
                  pl.BlockSpec((None, SUBLANE, c), lambda i, j: (i, jnp.minimum((j + 1) * per, last), 0)),
                  _resident((3, c)), _resident((1, c))],
        out_specs=pl.BlockSpec((None, c // LANE, None, tt, LANE), lambda i, j: (i, 0, j, 0, 0)),
        compiler_params=_params(("parallel", "parallel")),
        name="hyena_shortconv",
    )(hy_raw, hy_raw, hy_raw, p["short_w"], p["short_b"])


def _s5_kernel(u_ref, m_ref, ein_ref, gout_ref, lam_ref, d_ref, y_ref, ucat_ref, st_ref, *, nch, rb):
    sl = S5_STATE_LANES
    for t in range(S5_CHUNK):
        ucat_ref[:, t * LANE:(t + 1) * LANE] = u_ref[pl.ds(t, nch, stride=S5_CHUNK), :].astype(BF16)

    def state_in(i, carry):
        r0 = pl.multiple_of(i * rb, rb)
        lhs = ucat_ref[pl.ds(r0, rb), :]
        for cb in range(4):
            st_ref[pl.ds(r0, rb), cb * sl:(cb + 1) * sl] = jnp.dot(
                lhs, ein_ref[:, cb * sl:(cb + 1) * sl], preferred_element_type=F32)
        return carry

    lax.fori_loop(0, nch // rb, state_in, 0)

    lfr, lfi = lam_ref[0:1, :], lam_ref[1:2, :]
    lbr, lbi = lam_ref[2:3, :], lam_ref[3:4, :]

    def scan(i, carry):
        sfr, sfi, sbr, sbi = carry
        j = nch - 1 - i
        xfr = st_ref[pl.ds(i, 1), 0:sl]
        xfi = st_ref[pl.ds(i, 1), sl:2 * sl]
        st_ref[pl.ds(i, 1), 0:sl] = sfr
        st_ref[pl.ds(i, 1), sl:2 * sl] = sfi
        xbr = st_ref[pl.ds(j, 1), 2 * sl:3 * sl]
        xbi = st_ref[pl.ds(j, 1), 3 * sl:4 * sl]
        st_ref[pl.ds(j, 1), 2 * sl:3 * sl] = sbr
        st_ref[pl.ds(j, 1), 3 * sl:4 * sl] = sbi
        return (lfr * sfr - lfi * sfi + xfr, lfr * sfi + lfi * sfr + xfi,
                lbr * sbr - lbi * sbi + xbr, lbr * sbi + lbi * sbr + xbi)

    zero = jnp.zeros((1, sl), F32)
    lax.fori_loop(0, nch, scan, (zero, zero, zero, zero))

    def emit(i, carry):
        r0 = pl.multiple_of(i * rb, rb)
        lhs_u = ucat_ref[pl.ds(r0, rb), :]
        lhs_s = st_ref[pl.ds(r0, rb), :].astype(BF16)
        for cb in range(4):
            cols = slice(cb * 4 * LANE, (cb + 1) * 4 * LANE)
            yc = (jnp.dot(lhs_u, m_ref[:, cols], preferred_element_type=F32)
                  + jnp.dot(lhs_s, gout_ref[:, cols], preferred_element_type=F32))
            for tt in range(4):
                rows = pl.ds(r0 * S5_CHUNK + cb * 4 + tt, rb, stride=S5_CHUNK)
                y_ref[rows, :] = yc[:, tt * LANE:(tt + 1) * LANE] + d_ref[...] * u_ref[rows, :]
        return carry

    lax.fori_loop(0, nch // rb, emit, 0)


def _s5_call(u, p):
    b, l, _ = u.shape
    nch = l // S5_CHUNK
    rb = min(nch, 256)
    nq = D_SSM // LANE
    mat = pl.BlockSpec((None, S5_LANES, S5_LANES), lambda q, i: (q, 0, 0), pipeline_mode=pl.Buffered(1))
    return pl.pallas_call(
        functools.partial(_s5_kernel, nch=nch, rb=rb),
        out_shape=jax.ShapeDtypeStruct((b, l, D_SSM), F32),
        grid=(nq, b),
        in_specs=[pl.BlockSpec((None, l, LANE), lambda q, i: (i, 0, q)),
                  mat, mat, mat,
                  pl.BlockSpec((None, 4, S5_STATE_LANES), lambda q, i: (q, 0, 0)),
                  pl.BlockSpec((None, 1, LANE), lambda q, i: (q, 0, 0))],
        out_specs=pl.BlockSpec((None, l, LANE), lambda q, i: (i, 0, q)),
        scratch_shapes=[pltpu.VMEM((nch, S5_LANES), BF16), pltpu.VMEM((nch, 4 * S5_STATE_LANES), F32)],
        compiler_params=_params(("parallel", "parallel")),
        name="s5_chunked",
    )(u, p["s5_m"], p["s5_ein"], p["s5_gout"], p["s5_lam"], p["s5_d"])


def _cmul(ar, ai, br, bi):
    return ar * br - ai * bi, ar * bi + ai * br


def _s5_tables(lam_re, lam_im, log_step, b_re, b_im, c_re, c_im, d):
    g, hh, pp, tc = N_SSM_GROUPS, SSM_GROUP, SSM_STATE, S5_CHUNK
    gl = LANE // hh
    nq = g // gl
    step = jnp.exp(log_step)[..., None]
    mag = jnp.exp(lam_re * step)
    ar = mag * jnp.cos(lam_im * step)
    ai = mag * jnp.sin(lam_im * step)
    nr, ni = ar - 1.0, ai
    den = lam_re * lam_re + lam_im * lam_im
    qr = (nr * lam_re + ni * lam_im) / den
    qi = (ni * lam_re - nr * lam_im) / den
    bbr = qr[..., None] * b_re - qi[..., None] * b_im
    bbi = qr[..., None] * b_im + qi[..., None] * b_re
    pr, pi = [jnp.ones_like(ar)], [jnp.zeros_like(ai)]
    for _ in range(tc):
        r, i = _cmul(pr[-1], pi[-1], ar, ai)
        pr.append(r)
        pi.append(i)
    pwr, pwi = jnp.stack(pr), jnp.stack(pi)
    exact = lax.Precision.HIGHEST

    def c_times_pow(dd, sel):
        return _cmul(c_re[dd][None], c_im[dd][None], pwr[sel, dd][:, :, None, :], pwi[sel, dd][:, :, None, :])

    def lag_kernels(dd):
        zr, zi = c_times_pow(dd, jnp.arange(tc))
        return (jnp.einsum("jgap,gpb->jgab", zr, bbr[dd], precision=exact)
                - jnp.einsum("jgap,gpb->jgab", zi, bbi[dd], precision=exact))

    kf, kb = lag_kernels(0), lag_kernels(1)
    kall = jnp.concatenate([kb[jnp.arange(tc - 1, 0, -1)], kf[:1] + kb[:1], kf[1:]], axis=0)
    def expand(compact, src_col, row_grp, col_grp):
        rep = (jnp.arange(compact.shape[1])[:, None] == src_col[None, :]).astype(BF16)
        full = jnp.dot(compact.astype(BF16), rep, preferred_element_type=BF16)
        return jnp.where(row_grp[:, None] == col_grp[None, :], full, jnp.zeros((), BF16))

    lane = jnp.arange(LANE)
    chunk_lane = jnp.arange(S5_LANES)
    state_lane = jnp.arange(gl * pp)

    k2 = kall.transpose(0, 1, 3, 2).reshape((2 * tc - 1) * g * hh, hh)
    blk = expand(k2, lane % hh, (jnp.arange(k2.shape[0]) % LANE) // hh, lane // hh)
    blk = blk.reshape(2 * tc - 1, nq, LANE, LANE)
    tok = jnp.arange(tc)
    m = blk[tok[None, :] - tok[:, None] + tc - 1]
    m = m.transpose(2, 0, 3, 1, 4).reshape(nq, S5_LANES, S5_LANES)

    def tile_in(coef):
        c2 = coef.reshape(tc, nq, LANE, pp).transpose(1, 0, 2, 3).reshape(nq * S5_LANES, pp)
        full = expand(c2, state_lane % pp, (jnp.arange(nq * S5_LANES) % LANE) // hh, state_lane // pp)
        return full.reshape(nq, S5_LANES, gl * pp)

    bbr_t, bbi_t = bbr.transpose(0, 1, 3, 2), bbi.transpose(0, 1, 3, 2)

    def state_in(dd, sel):
        return _cmul(pwr[sel, dd][:, :, None, :], pwi[sel, dd][:, :, None, :], bbr_t[dd][None], bbi_t[dd][None])

    ein = jnp.concatenate([tile_in(c) for c in state_in(0, jnp.arange(tc - 1, -1, -1))]
                          + [tile_in(c) for c in state_in(1, jnp.arange(tc))], axis=2)

    def tile_out(coef):
        c2 = coef.transpose(1, 3, 0, 2).reshape(g * pp, tc * hh)
        full = expand(c2, (chunk_lane // LANE) * hh + chunk_lane % hh,
                      (jnp.arange(g * pp) % (gl * pp)) // pp, (chunk_lane % LANE) // hh)
        return full.reshape(nq, gl * pp, S5_LANES)

    ofr, ofi = c_times_pow(0, jnp.arange(1, tc + 1))
    obr, obi = c_times_pow(1, jnp.arange(tc, 0, -1))
    gout = jnp.concatenate([tile_out(ofr), tile_out(-ofi), tile_out(obr), tile_out(-obi)], axis=1)

    lam16 = jnp.stack([pwr[tc, 0], pwi[tc, 0], pwr[tc, 1], pwi[tc, 1]], axis=0)
    lam16 = lam16.reshape(4, nq, gl * pp).transpose(1, 0, 2)
    return dict(s5_m=m, s5_ein=ein, s5_gout=gout,
                s5_lam=lam16.astype(F32), s5_d=d.reshape(nq, 1, LANE).astype(F32))


def _fft_dims(l):
    n = 2 * l
    bits = n.bit_length() - 1
    assert n == 1 << bits and bits % 2 == 0, "sequence length must give a square power-of-two DFT size"
    n1 = 1 << (bits // 2)
    hp = -(-(n1 // 2 + 1) // SUBLANE) * SUBLANE
    return n1, n1, hp


def _t2_rows(t2):
    return pl.ds(pl.multiple_of(t2 * SUBLANE, SUBLANE), SUBLANE)


def _a_rows(t2hi, t2lo, hp):
    return pl.ds(t2hi * (hp * SUBLANE) + t2lo, hp, stride=SUBLANE)


def _a_tile(t2hi, k1, hp):
    return pl.ds(pl.multiple_of((t2hi * hp + k1) * SUBLANE, SUBLANE), SUBLANE)


def _a_load_k1(ref, k1, n2, hp):
    return jnp.concatenate([ref[_a_tile(t, k1, hp), :] for t in range(n2 // SUBLANE)], axis=0)


def _a_store_k1(ref, k1, val, n2, hp):
    for t in range(n2 // SUBLANE):
        ref[_a_tile(t, k1, hp), :] = val[t * SUBLANE:(t + 1) * SUBLANE, :]


def _k1_block(hp):
    return max(d for d in range(2, K1_BLOCK_MAX + 1, 2) if hp % d == 0)


def _dft_tables(l):
    n = 2 * l
    n1, n2, hp = _fft_dims(l)
    t2 = jnp.arange(n2, dtype=jnp.int32)[:, None, None]
    k1 = jnp.arange(hp, dtype=jnp.int32)[None, :, None]
    t1 = jnp.arange(n1, dtype=jnp.int32)[None, None, :]
    ang = (2.0 * math.pi / n) * (((t1 * n2 + t2) * k1) % n).astype(F32)
    cos1, sin1 = jnp.cos(ang), jnp.sin(ang)
    e1 = jnp.concatenate([cos1, -sin1], axis=1)
    wgt = jnp.where((k1 == 0) | (k1 == n1 // 2), 1.0, jnp.where(k1 < n1 // 2, 2.0, 0.0))
    half = n1 // 2
    einv = jnp.concatenate([(wgt * cos1)[:, :, :half], (-wgt * sin1)[:, :, :half]], axis=1)
    einv = einv.transpose(0, 2, 1)
    a = jnp.arange(n2, dtype=jnp.int32)
    ang2 = (2.0 * math.pi / n2) * ((a[:, None] * a[None, :]) % n2).astype(F32)
    c2, s2 = jnp.cos(ang2), jnp.sin(ang2)
    f2c = jnp.block([[c2, s2], [-s2, c2]])
    f2ic = jnp.block([[c2, -s2], [s2, c2]])
    return dict(e1=e1.astype(BF16), einv=einv.astype(BF16), f2c=f2c.astype(BF16), f2ic=f2ic.astype(BF16))


def _filter_kernel(w1_ref, b1_ref, w2_ref, b2_ref, sf_ref, om_ref, w3f_ref, w3b_ref, ldf_ref, ldb_ref,
                   e1_ref, f2c_ref, kr_ref, ki_ref, hdn_ref, are_ref, aim_ref, scale_ref, *, l, n1, n2, hp, k1b):
    j = pl.program_id(0)
    kb = pl.program_id(1)
    n = 2 * l

    def positions(t2):
        t1 = lax.broadcasted_iota(jnp.int32, (n1, LANE), 0)
        i = t1 * n2 + t2
        return i, jnp.where(i <= l, i, n - i).astype(F32)

    @pl.when((j == 0) & (kb == 0))
    def _():
        def body(t2, carry):
            _, pos = positions(t2)
            ang = pos * om_ref[...]
            lane = lax.broadcasted_iota(jnp.int32, (n1, LANE), 1)
            feats = jnp.where(lane == 0, pos / HYENA_TIME_SCALE,
                              jnp.where(lane <= HYENA_BANDS, jnp.sin(ang),
                                        jnp.where(lane <= 2 * HYENA_BANDS, jnp.cos(ang), 0.0)))
            h1 = jnp.sin(sf_ref[0:1, :] * (jnp.dot(feats, w1_ref[...], preferred_element_type=F32) + b1_ref[...]))
            h2 = jnp.sin(sf_ref[1:2, :] * (jnp.dot(h1, w2_ref[...], preferred_element_type=F32) + b2_ref[...]))
            hdn_ref[pl.ds(pl.multiple_of(t2 * n1, n1), n1), :] = h2
            return carry
        lax.fori_loop(0, n2, body, 0)

    @pl.when(kb == 0)
    def _():
        def body(t2hi, asum):
            for t2lo in range(SUBLANE):
                t2 = t2hi * SUBLANE + t2lo
                i, pos = positions(t2)
                h2 = hdn_ref[pl.ds(pl.multiple_of(t2 * n1, n1), n1), :].astype(BF16)
                t_lin = pos / HYENA_TIME_SCALE
                fwd = jnp.dot(h2, w3f_ref[...], preferred_element_type=F32) * jnp.exp(-t_lin * jnp.exp(ldf_ref[...]))
                bwd = jnp.dot(h2, w3b_ref[...], preferred_element_type=F32) * jnp.exp(-t_lin * jnp.exp(ldb_ref[...]))
                k = jnp.where(i < l, fwd, jnp.where(i > l, bwd, 0.0))
                r = jnp.dot(e1_ref[t2], k.astype(BF16), preferred_element_type=F32)
                are_ref[_a_rows(t2hi, t2lo, hp), :] = r[:hp]
                aim_ref[_a_rows(t2hi, t2lo, hp), :] = r[hp:]
                asum = asum + jnp.sum(jnp.abs(k), axis=0, keepdims=True)
            return asum
        asum = lax.fori_loop(0, n2 // SUBLANE, body, jnp.zeros((1, LANE), F32))
        scale_ref[...] = 1.0 / ((asum + FILTER_EPS) * n)

    for kk in range(0, k1b, 2):
        k1 = kb * k1b + kk
        pair = lambda ref: jnp.concatenate([_a_load_k1(ref, k1 + e, n2, hp) for e in range(2)], axis=1)
        rhs = jnp.concatenate([pair(are_ref), pair(aim_ref)], axis=0).astype(BF16)
        x = jnp.dot(f2c_ref[...], rhs, preferred_element_type=F32)
        for e in range(2):
            rows = slice((kk + e) * n2, (kk + e + 1) * n2)
            kr_ref[rows, :] = x[:n2, e * LANE:(e + 1) * LANE] * scale_ref[...]
            ki_ref[rows, :] = x[n2:, e * LANE:(e + 1) * LANE] * scale_ref[...]


def _filter_call(l, p, tb):
    n1, n2, hp = _fft_dims(l)
    k1b = _k1_block(hp)
    n = 2 * l
    nt = HYENA_ORDER * D_HYENA // LANE
    per = D_HYENA // LANE
    fwd_col = lambda j, kb: (0, (j // per) * 2 * per + j % per)
    bwd_col = lambda j, kb: (0, (j // per) * 2 * per + per + j % per)
    spec_out = pl.BlockSpec((k1b * n2, LANE), lambda j, kb: (kb, j))
    a_shape = (n2 * hp, LANE)
    return pl.pallas_call(
        functools.partial(_filter_kernel, l=l, n1=n1, n2=n2, hp=hp, k1b=k1b),
        out_shape=(jax.ShapeDtypeStruct((hp * n2, nt * LANE), F32),) * 2,
        grid=(nt, hp // k1b),
        in_specs=[_resident((LANE, LANE)), _resident((1, LANE)), _resident((LANE, LANE)), _resident((1, LANE)),
                  _resident((2, LANE)), _resident((1, LANE)),
                  pl.BlockSpec((LANE, LANE), fwd_col), pl.BlockSpec((LANE, LANE), bwd_col),
                  pl.BlockSpec((1, LANE), fwd_col), pl.BlockSpec((1, LANE), bwd_col),
                  _resident((n2, 2 * hp, n1)), _resident((2 * n2, 2 * n2))],
        out_specs=(spec_out, spec_out),
        scratch_shapes=[pltpu.VMEM((n, LANE), F32), pltpu.VMEM(a_shape, F32),
                        pltpu.VMEM(a_shape, F32), pltpu.VMEM((1, LANE), F32)],
        compiler_params=_params(("arbitrary", "arbitrary")),
        name="hyena_filter_spectrum",
    )(p["filt_w1"], p["filt_b1"], p["filt_w2"], p["filt_b2"], p["filt_sf"], p["filt_om"],
      p["filt_w3"], p["filt_w3"], p["filt_ld"], p["filt_ld"], tb["e1"], tb["f2c"])


def _conv_kernel(z_ref, g_ref, bias_ref, e1_ref, einv_ref, f2c_ref, f2ic_ref, kr_ref, ki_ref, o_ref,
                 are_ref, aim_ref, *, n1, n2, hp, k1b):
    s = pl.program_id(2)
    half = n1 // 2

    @pl.when(s == 0)
    def _():
        def body(t2hi, carry):
            for t2lo in range(SUBLANE):
                t2 = t2hi * SUBLANE + t2lo
                z = z_ref[:, _t2_rows(t2), :].reshape(half, LANE).astype(BF16)
                r = jnp.dot(e1_ref[t2, :, 0:half], z, preferred_element_type=F32)
                are_ref[_a_rows(t2hi, t2lo, hp), :] = r[:hp]
                aim_ref[_a_rows(t2hi, t2lo, hp), :] = r[hp:]
            return carry
        lax.fori_loop(0, n2 // SUBLANE, body, 0, unroll=2)

    for kk in range(0, k1b, 2):
        k1 = s * k1b + kk
        pair = lambda ref: jnp.concatenate([_a_load_k1(ref, k1 + e, n2, hp) for e in range(2)], axis=1)
        rhs = jnp.concatenate([pair(are_ref), pair(aim_ref)], axis=0).astype(BF16)
        x = jnp.dot(f2c_ref[...], rhs, preferred_element_type=F32)
        xr, xi = x[:n2], x[n2:]
        spec = lambda ref: jnp.concatenate([ref[(kk + e) * n2:(kk + e + 1) * n2, :] for e in range(2)], axis=1)
        fr, fi = spec(kr_ref), spec(ki_ref)
        y = jnp.concatenate([xr * fr - xi * fi, xr * fi + xi * fr], axis=0).astype(BF16)
        back = jnp.dot(f2ic_ref[...], y, preferred_element_type=F32)
        for e in range(2):
            _a_store_k1(are_ref, k1 + e, back[:n2, e * LANE:(e + 1) * LANE], n2, hp)
            _a_store_k1(aim_ref, k1 + e, back[n2:, e * LANE:(e + 1) * LANE], n2, hp)

    @pl.when(s == pl.num_programs(2) - 1)
    def _():
        def body(t2hi, carry):
            for t2lo in range(SUBLANE):
                t2 = t2hi * SUBLANE + t2lo
                rows = _a_rows(t2hi, t2lo, hp)
                rhs = jnp.concatenate([are_ref[rows, :], aim_ref[rows, :]], axis=0).astype(BF16)
                y = jnp.dot(einv_ref[t2], rhs, preferred_element_type=F32)
                t1 = (half // SUBLANE, SUBLANE, LANE)
                out = g_ref[:, _t2_rows(t2), :] * (y.reshape(t1) + bias_ref[...] * z_ref[:, _t2_rows(t2), :])
                o_ref[:, _t2_rows(t2), :] = out
            return carry
        lax.fori_loop(0, n2 // SUBLANE, body, 0, unroll=2)


def _conv_call(z, z_col0, g, g_col0, bias, kr, ki, order, tb):
    b, _, t1_tiles, tt, _ = z.shape
    n2, half = tt // SUBLANE, t1_tiles * SUBLANE
    n1, _, hp = _fft_dims(n2 * half)
    k1b = _k1_block(hp)
    per = D_HYENA // LANE
    seq = lambda col0: pl.BlockSpec((None, None, t1_tiles, tt, LANE), lambda c, i, s: (i, col0 + c, 0, 0, 0))
    spec_k = pl.BlockSpec((k1b * n2, LANE), lambda c, i, s: (s, order * per + c))
    a_shape = (n2 * hp, LANE)
    return pl.pallas_call(
        functools.partial(_conv_kernel, n1=n1, n2=n2, hp=hp, k1b=k1b),
        out_shape=jax.ShapeDtypeStruct((b, per, t1_tiles, tt, LANE), F32),
        grid=(per, b, hp // k1b),
        in_specs=[seq(z_col0), seq(g_col0),
                  pl.BlockSpec((1, LANE), lambda c, i, s: (0, order * per + c)),
                  _resident((n2, 2 * hp, n1)), _resident((n2, n1 // 2, 2 * hp)),
                  _resident((2 * n2, 2 * n2)), _resident((2 * n2, 2 * n2)),
                  spec_k, spec_k],
        out_specs=pl.BlockSpec((None, None, t1_tiles, tt, LANE), lambda c, i, s: (i, c, 0, 0, 0)),
        scratch_shapes=[pltpu.VMEM(a_shape, F32), pltpu.VMEM(a_shape, F32)],
        compiler_params=_params(("parallel", "parallel", "arbitrary")),
        name=f"hyena_conv_order{order}",
    )(z, g, bias, tb["e1"], tb["einv"], tb["f2c"], tb["f2ic"], kr, ki)


def _prepare(ffn1_w_gate, ffn1_w_up, ffn1_w_down, ln1_g, ln1_b, w_in,
             ssm_lam_re, ssm_lam_im, ssm_log_step, ssm_b_re, ssm_b_im, ssm_c_re, ssm_c_im,
             ssm_d, ssm_glu_w, ssm_glu_b, ssm_norm_g,
             hy_short_w, hy_short_b, hy_filt_w1, hy_filt_b1, hy_filt_w2, hy_filt_b2, hy_filt_w3,
             hy_sin_freq, hy_log_decay, hy_bias, hy_norm_g, w_out, ln2_g, ln2_b,
             ffn2_w_gate, ffn2_w_up, ffn2_w_down, ln3_g, ln3_b):
    row = lambda a: a[0].reshape(1, -1).astype(F32)
    hid = HYENA_FILTER_HIDDEN
    pad2 = lambda a, r, c: jnp.zeros((r, c), F32).at[:a.shape[0], :a.shape[1]].set(a.astype(F32))
    omega = jnp.exp(-math.log(HYENA_MAX_PERIOD) * jnp.arange(HYENA_BANDS, dtype=F32) / HYENA_BANDS)
    om = jnp.zeros((1, LANE), F32).at[0, 1:1 + HYENA_BANDS].set(omega).at[0, 1 + HYENA_BANDS:HYENA_POS_DIM].set(omega)
    p = dict(
        ffn1_wg=ffn1_w_gate[0].astype(BF16), ffn1_wu=ffn1_w_up[0].astype(BF16), ffn1_wd=ffn1_w_down[0].astype(BF16),
        ln1_g=row(ln1_g), ln1_b=row(ln1_b), w_in=w_in[0].astype(BF16),
        glu_w=ssm_glu_w[0].astype(BF16), glu_b=row(ssm_glu_b), ssm_norm_g=row(ssm_norm_g),
        hy_norm_g=row(hy_norm_g), w_out=w_out[0].astype(BF16), ln2_g=row(ln2_g), ln2_b=row(ln2_b),
        ffn2_wg=ffn2_w_gate[0].astype(BF16), ffn2_wu=ffn2_w_up[0].astype(BF16), ffn2_wd=ffn2_w_down[0].astype(BF16),
        ln3_g=row(ln3_g), ln3_b=row(ln3_b),
        short_w=hy_short_w[0].astype(F32), short_b=row(hy_short_b),
        filt_w1=pad2(hy_filt_w1[0], LANE, LANE), filt_b1=pad2(hy_filt_b1[0].reshape(1, hid), 1, LANE),
        filt_w2=pad2(hy_filt_w2[0], LANE, LANE), filt_b2=pad2(hy_filt_b2[0].reshape(1, hid), 1, LANE),
        filt_sf=pad2(hy_sin_freq[0], 2, LANE), filt_om=om,
        filt_w3=pad2(hy_filt_w3[0], LANE, hy_filt_w3.shape[-1]).astype(BF16), filt_ld=row(hy_log_decay),
        hy_bias=hy_bias[0].reshape(1, HYENA_ORDER * D_HYENA).astype(F32),
    )
    f32 = lambda a: a[0].astype(F32)
    p.update(_s5_tables(f32(ssm_lam_re), f32(ssm_lam_im), f32(ssm_log_step), f32(ssm_b_re), f32(ssm_b_im),
                        f32(ssm_c_re), f32(ssm_c_im), f32(ssm_d)))
    return p


def _trunk(x, p):
    b, l, d = x.shape
    t = b * l
    per = D_HYENA // LANE
    h1, u, hy_raw = _ffn1_call(x.reshape(t, d), p)
    hy = _shortconv_call(hy_raw.reshape(b, l, D_HY_IN), p)
    y_ssm = _s5_call(u.reshape(b, l, D_SSM), p)
    tb = _dft_tables(l)
    kr, ki = _filter_call(l, p, tb)
    z1 = _conv_call(hy, 0, hy, per, p["hy_bias"], kr, ki, 0, tb)
    z2 = _conv_call(z1, 0, hy, 2 * per, p["hy_bias"], kr, ki, 1, tb)
    return _final_call(y_ssm, z2, h1.reshape(b, l, d), p)


def kernel(x_prompt, x_sample, ffn1_w_gate, ffn1_w_up, ffn1_w_down, ln1_g, ln1_b, w_in, ssm_lam_re, ssm_lam_im, ssm_log_step, ssm_b_re, ssm_b_im, ssm_c_re, ssm_c_im, ssm_d, ssm_glu_w, ssm_glu_b, ssm_norm_g, hy_short_w, hy_short_b, hy_filt_w1, hy_filt_b1, hy_filt_w2, hy_filt_b2, hy_filt_w3, hy_sin_freq, hy_log_decay, hy_bias, hy_norm_g, w_out, ln2_g, ln2_b, ffn2_w_gate, ffn2_w_up, ffn2_w_down, ln3_g, ln3_b):
    p = _prepare(ffn1_w_gate, ffn1_w_up, ffn1_w_down, ln1_g, ln1_b, w_in,
                 ssm_lam_re, ssm_lam_im, ssm_log_step, ssm_b_re, ssm_b_im, ssm_c_re, ssm_c_im,
                 ssm_d, ssm_glu_w, ssm_glu_b, ssm_norm_g,
                 hy_short_w, hy_short_b, hy_filt_w1, hy_filt_b1, hy_filt_w2, hy_filt_b2, hy_filt_w3,
                 hy_sin_freq, hy_log_decay, hy_bias, hy_norm_g, w_out, ln2_g, ln2_b,
                 ffn2_w_gate, ffn2_w_up, ffn2_w_down, ln3_g, ln3_b)
    return (_trunk(x_prompt, p), _trunk(x_sample, p))
```

```python
import functools
import math

import jax
import jax.numpy as jnp
from jax import lax
from jax.experimental import pallas as pl
from jax.experimental.pallas import tpu as pltpu

F32 = jnp.float32
BF16 = jnp.bfloat16

D_MODEL = 1024
D_SSM = 512
SSM_GROUP = 16
N_SSM_GROUPS = D_SSM // SSM_GROUP
SSM_STATE = 64
D_HYENA = 512
HYENA_ORDER = 2
HYENA_BANDS = 8
HYENA_POS_DIM = 1 + 2 * HYENA_BANDS
HYENA_FILTER_HIDDEN = 64
HYENA_TIME_SCALE = 4096.0
HYENA_MAX_PERIOD = 10000.0
D_HY_IN = (HYENA_ORDER + 1) * D_HYENA
D_IN = D_SSM + D_HY_IN
D_FF = 128 * math.ceil(8 * D_MODEL / 3 / 128)
LN_EPS = 1e-5
RMS_EPS = 1e-6
FILTER_EPS = 1e-6
DEPTH = 1
DEEPNORM_ALPHA = (2.0 * DEPTH) ** 0.25

LANE = 128
SUBLANE = 8
V7X_VMEM_BYTES = 64 * 1024 * 1024
VMEM_LIMIT_BYTES = V7X_VMEM_BYTES - 8 * 1024 * 1024

TOKEN_TILE = 512
FF_CHUNK = 256
S5_CHUNK = 16
S5_LANES = S5_CHUNK * LANE
S5_STATE_LANES = (LANE // SSM_GROUP) * SSM_STATE
K1_BLOCK_MAX = 24


def _params(semantics):
    return pltpu.CompilerParams(dimension_semantics=semantics, vmem_limit_bytes=VMEM_LIMIT_BYTES)


def _resident(shape):
    nd = len(shape)
    return pl.BlockSpec(shape, lambda *_: (0,) * nd, pipeline_mode=pl.Buffered(1))


def _layer_norm(x, g, b):
    mu = jnp.mean(x, axis=-1, keepdims=True)
    xc = x - mu
    var = jnp.mean(xc * xc, axis=-1, keepdims=True)
    return xc * lax.rsqrt(var + LN_EPS) * g + b


def _rms_norm(x, g):
    ms = jnp.mean(x * x, axis=-1, keepdims=True)
    return x * lax.rsqrt(ms + RMS_EPS) * g


def _ffn_ln(x, wg_ref, wu_ref, wd_ref, g_ref, b_ref, acc_ref):
    xb = x.astype(BF16)
    acc_ref[...] = jnp.zeros_like(acc_ref)

    def body(c, carry):
        off = pl.multiple_of(c * FF_CHUNK, FF_CHUNK)
        gate = jnp.dot(xb, wg_ref[:, pl.ds(off, FF_CHUNK)], preferred_element_type=F32)
        up = jnp.dot(xb, wu_ref[:, pl.ds(off, FF_CHUNK)], preferred_element_type=F32)
        act = (jax.nn.silu(gate) * up).astype(BF16)
        acc_ref[...] += jnp.dot(act, wd_ref[pl.ds(off, FF_CHUNK), :], preferred_element_type=F32)
        return carry

    lax.fori_loop(0, D_FF // FF_CHUNK, body, 0, unroll=True)
    return _layer_norm(DEEPNORM_ALPHA * x + 0.5 * acc_ref[...], g_ref[...], b_ref[...])


def _ffn1_kernel(x_ref, wg_ref, wu_ref, wd_ref, g_ref, b_ref, win_ref, h_ref, u_ref, hy_ref, acc_ref, *, n2):
    h = _ffn_ln(x_ref[...], wg_ref, wu_ref, wd_ref, g_ref, b_ref, acc_ref)
    h_ref[...] = h
    hb = h.astype(BF16)
    u_ref[...] = jnp.dot(hb, win_ref[:, :D_SSM], preferred_element_type=F32)
    hy = jnp.dot(hb, win_ref[:, D_SSM:], preferred_element_type=F32)
    per = TOKEN_TILE // n2
    first = (pl.program_id(0) % (SUBLANE // per)) * per
    for ct in range(D_HY_IN // LANE):
        for t1 in range(per):
            hy_ref[ct, pl.ds(first + t1, n2, stride=SUBLANE), :] = hy[t1 * n2:(t1 + 1) * n2, ct * LANE:(ct + 1) * LANE]


def _ffn1_call(x, p):
    b, l, _ = x.shape
    _, n2, _ = _fft_dims(l)
    tm = TOKEN_TILE
    assert l % tm == 0 and tm % n2 == 0 and SUBLANE % (tm // n2) == 0
    t = b * l
    steps = SUBLANE * n2 // tm
    tiles = l // tm
    row = lambda w: pl.BlockSpec((tm, w), lambda i: (i, 0))
    return pl.pallas_call(
        functools.partial(_ffn1_kernel, n2=n2),
        out_shape=(jax.ShapeDtypeStruct((t, D_MODEL), F32),
                   jax.ShapeDtypeStruct((t, D_SSM), F32),
                   jax.ShapeDtypeStruct((b, D_HY_IN // LANE, l // (SUBLANE * n2), SUBLANE * n2, LANE), F32)),
        grid=(t // tm,),
        in_specs=[row(D_MODEL), _resident((D_MODEL, D_FF)), _resident((D_MODEL, D_FF)),
                  _resident((D_FF, D_MODEL)), _resident((1, D_MODEL)), _resident((1, D_MODEL)),
                  _resident((D_MODEL, D_IN))],
        out_specs=(row(D_MODEL), row(D_SSM),
                   pl.BlockSpec((None, D_HY_IN // LANE, None, SUBLANE * n2, LANE),
                                lambda i: (i // tiles, 0, (i % tiles) // steps, 0, 0))),
        scratch_shapes=[pltpu.VMEM((tm, D_MODEL), F32)],
        compiler_params=_params(("arbitrary",)),
        name="ffn1_ln1_proj",
    )(x.reshape(t, D_MODEL), p["ffn1_wg"], p["ffn1_wu"], p["ffn1_wd"], p["ln1_g"], p["ln1_b"], p["w_in"])


def _final_kernel(ys_ref, yh_ref, h_ref, gw_ref, gb_ref, sg_ref, hg_ref, wo_ref, l2g_ref, l2b_ref,
                  wg_ref, wu_ref, wd_ref, l3g_ref, l3b_ref, o_ref, acc_ref, *, n2):
    per = TOKEN_TILE // n2

    def sub_tile(hh, carry):
        rows = pl.ds(pl.multiple_of(hh * TOKEN_TILE, TOKEN_TILE), TOKEN_TILE)
        y_hy = jnp.concatenate(
            [jnp.concatenate([yh_ref[ct, pl.ds(hh * per + jj, n2, stride=SUBLANE), :]
                              for ct in range(D_HYENA // LANE)], axis=1) for jj in range(per)], axis=0)
        g = jax.nn.gelu(ys_ref[rows, :])
        gate = jax.nn.sigmoid(jnp.dot(g.astype(BF16), gw_ref[...], preferred_element_type=F32) + gb_ref[...])
        y_ssm = _rms_norm(g * gate, sg_ref[...])
        y_hy = _rms_norm(y_hy, hg_ref[...])
        mix = (jnp.dot(y_ssm.astype(BF16), wo_ref[:D_SSM, :], preferred_element_type=F32)
               + jnp.dot(y_hy.astype(BF16), wo_ref[D_SSM:, :], preferred_element_type=F32))
        x2 = _layer_norm(DEEPNORM_ALPHA * h_ref[rows, :] + mix, l2g_ref[...], l2b_ref[...])
        o_ref[rows, :] = _ffn_ln(x2, wg_ref, wu_ref, wd_ref, l3g_ref, l3b_ref, acc_ref)
        return carry

    lax.fori_loop(0, SUBLANE * n2 // TOKEN_TILE, sub_tile, 0)


def _final_call(y_ssm, y_hy, h1, p):
    b, l, _ = y_ssm.shape
    tt = y_hy.shape[3]
    n2 = tt // SUBLANE
    assert tt % TOKEN_TILE == 0 and TOKEN_TILE % n2 == 0
    row = lambda w: pl.BlockSpec((None, tt, w), lambda i, a: (i, a, 0))
    return pl.pallas_call(
        functools.partial(_final_kernel, n2=n2),
        out_shape=jax.ShapeDtypeStruct((b, l, D_MODEL), F32),
        grid=(b, l // tt),
        in_specs=[row(D_SSM),
                  pl.BlockSpec((None, D_HYENA // LANE, None, tt, LANE), lambda i, a: (i, 0, a, 0, 0)),
                  row(D_MODEL),
                  _resident((D_SSM, D_SSM)), _resident((1, D_SSM)), _resident((1, D_SSM)),
                  _resident((1, D_HYENA)), _resident((D_MODEL, D_MODEL)),
                  _resident((1, D_MODEL)), _resident((1, D_MODEL)),
                  _resident((D_MODEL, D_FF)), _resident((D_MODEL, D_FF)), _resident((D_FF, D_MODEL)),
                  _resident((1, D_MODEL)), _resident((1, D_MODEL))],
        out_specs=row(D_MODEL),
        scratch_shapes=[pltpu.VMEM((TOKEN_TILE, D_MODEL), F32)],
        compiler_params=_params(("parallel", "parallel")),
        name="mix_ln2_ffn2_ln3",
    )(y_ssm, y_hy, h1, p["glu_w"], p["glu_b"], p["ssm_norm_g"], p["hy_norm_g"], p["w_out"],
      p["ln2_g"], p["ln2_b"], p["ffn2_wg"], p["ffn2_wu"], p["ffn2_wd"], p["ln3_g"], p["ln3_b"])


def _s5_kernel(u_ref, m_ref, ein_ref, gout_ref, lam_ref, d_ref, y_ref, ucat_ref, st_ref, *, nch, rb):
    sl = S5_STATE_LANES
    for t in range(S5_CHUNK):
        ucat_ref[:, t * LANE:(t + 1) * LANE] = u_ref[pl.ds(t, nch, stride=S5_CHUNK), :].astype(BF16)

    def state_in(i, carry):
        r0 = pl.multiple_of(i * rb, rb)
        lhs = ucat_ref[pl.ds(r0, rb), :]
        for cb in range(4):
            st_ref[pl.ds(r0, rb), cb * sl:(cb + 1) * sl] = jnp.dot(
                lhs, ein_ref[:, cb * sl:(cb + 1) * sl], preferred_element_type=F32)
        return carry

    lax.fori_loop(0, nch // rb, state_in, 0)

    lfr, lfi = lam_ref[0:1, :], lam_ref[1:2, :]
    lbr, lbi = lam_ref[2:3, :], lam_ref[3:4, :]

    def scan(i, carry):
        sfr, sfi, sbr, sbi = carry
        j = nch - 1 - i
        xfr = st_ref[pl.ds(i, 1), 0:sl]
        xfi = st_ref[pl.ds(i, 1), sl:2 * sl]
        st_ref[pl.ds(i, 1), 0:sl] = sfr
        st_ref[pl.ds(i, 1), sl:2 * sl] = sfi
        xbr = st_ref[pl.ds(j, 1), 2 * sl:3 * sl]
        xbi = st_ref[pl.ds(j, 1), 3 * sl:4 * sl]
        st_ref[pl.ds(j, 1), 2 * sl:3 * sl] = sbr
        st_ref[pl.ds(j, 1), 3 * sl:4 * sl] = sbi
        return (lfr * sfr - lfi * sfi + xfr, lfr * sfi + lfi * sfr + xfi,
                lbr * sbr - lbi * sbi + xbr, lbr * sbi + lbi * sbr + xbi)

    zero = jnp.zeros((1, sl), F32)
    lax.fori_loop(0, nch, scan, (zero, zero, zero, zero))

    def emit(i, carry):
        r0 = pl.multiple_of(i * rb, rb)
        lhs_u = ucat_ref[pl.ds(r0, rb), :]
        lhs_s = st_ref[pl.ds(r0, rb), :].astype(BF16)
        for cb in range(4):
            cols = slice(cb * 4 * LANE, (cb + 1) * 4 * LANE)
            yc = (jnp.dot(lhs_u, m_ref[:, cols], preferred_element_type=F32)
                  + jnp.dot(lhs_s, gout_ref[:, cols], preferred_element_type=F32))
            for tt in range(4):
                rows = pl.ds(r0 * S5_CHUNK + cb * 4 + tt, rb, stride=S5_CHUNK)
                y_ref[rows, :] = yc[:, tt * LANE:(tt + 1) * LANE] + d_ref[...] * u_ref[rows, :]
        return carry

    lax.fori_loop(0, nch // rb, emit, 0)


def _s5_call(u, p):
    b, l, _ = u.shape
    nch = l // S5_CHUNK
    rb = min(nch, 256)
    nq = D_SSM // LANE
    mat = pl.BlockSpec((None, S5_LANES, S5_LANES), lambda q, i: (q, 0, 0), pipeline_mode=pl.Buffered(1))
    return pl.pallas_call(
        functools.partial(_s5_kernel, nch=nch, rb=rb),
        out_shape=jax.ShapeDtypeStruct((b, l, D_SSM), F32),
        grid=(nq, b),
        in_specs=[pl.BlockSpec((None, l, LANE), lambda q, i: (i, 0, q)),
                  mat, mat, mat,
                  pl.BlockSpec((None, 4, S5_STATE_LANES), lambda q, i: (q, 0, 0)),
                  pl.BlockSpec((None, 1, LANE), lambda q, i: (q, 0, 0))],
        out_specs=pl.BlockSpec((None, l, LANE), lambda q, i: (i, 0, q)),
        scratch_shapes=[pltpu.VMEM((nch, S5_LANES), BF16), pltpu.VMEM((nch, 4 * S5_STATE_LANES), F32)],
        compiler_params=_params(("parallel", "parallel")),
        name="s5_chunked",
    )(u, p["s5_m"], p["s5_ein"], p["s5_gout"], p["s5_lam"], p["s5_d"])


def _cmul(ar, ai, br, bi):
    return ar * br - ai * bi, ar * bi + ai * br


def _s5_tables(lam_re, lam_im, log_step, b_re, b_im, c_re, c_im, d):
    g, hh, pp, tc = N_SSM_GROUPS, SSM_GROUP, SSM_STATE, S5_CHUNK
    gl = LANE // hh
    nq = g // gl
    step = jnp.exp(log_step)[..., None]
    mag = jnp.exp(lam_re * step)
    ar = mag * jnp.cos(lam_im * step)
    ai = mag * jnp.sin(lam_im * step)
    nr, ni = ar - 1.0, ai
    den = lam_re * lam_re + lam_im * lam_im
    qr = (nr * lam_re + ni * lam_im) / den
    qi = (ni * lam_re - nr * lam_im) / den
    bbr = qr[..., None] * b_re - qi[..., None] * b_im
    bbi = qr[..., None] * b_im + qi[..., None] * b_re
    pr, pi = [jnp.ones_like(ar)], [jnp.zeros_like(ai)]
    for _ in range(tc):
        r, i = _cmul(pr[-1], pi[-1], ar, ai)
        pr.append(r)
        pi.append(i)
    pwr, pwi = jnp.stack(pr), jnp.stack(pi)
    exact = lax.Precision.HIGHEST

    def c_times_pow(dd, sel):
        return _cmul(c_re[dd][None], c_im[dd][None], pwr[sel, dd][:, :, None, :], pwi[sel, dd][:, :, None, :])

    def lag_kernels(dd):
        zr, zi = c_times_pow(dd, jnp.arange(tc))
        return (jnp.einsum("jgap,gpb->jgab", zr, bbr[dd], precision=exact)
                - jnp.einsum("jgap,gpb->jgab", zi, bbi[dd], precision=exact))

    kf, kb = lag_kernels(0), lag_kernels(1)
    kall = jnp.concatenate([kb[jnp.arange(tc - 1, 0, -1)], kf[:1] + kb[:1], kf[1:]], axis=0)
    def expand(compact, src_col, row_grp, col_grp):
        rep = (jnp.arange(compact.shape[1])[:, None] == src_col[None, :]).astype(BF16)
        full = jnp.dot(compact.astype(BF16), rep, preferred_element_type=BF16)
        return jnp.where(row_grp[:, None] == col_grp[None, :], full, jnp.zeros((), BF16))

    lane = jnp.arange(LANE)
    chunk_lane = jnp.arange(S5_LANES)
    state_lane = jnp.arange(gl * pp)

    k2 = kall.transpose(0, 1, 3, 2).reshape((2 * tc - 1) * g * hh, hh)
    blk = expand(k2, lane % hh, (jnp.arange(k2.shape[0]) % LANE) // hh, lane // hh)
    blk = blk.reshape(2 * tc - 1, nq, LANE, LANE)
    tok = jnp.arange(tc)
    m = blk[tok[None, :] - tok[:, None] + tc - 1]
    m = m.transpose(2, 0, 3, 1, 4).reshape(nq, S5_LANES, S5_LANES)

    def tile_in(coef):
        c2 = coef.reshape(tc, nq, LANE, pp).transpose(1, 0, 2, 3).reshape(nq * S5_LANES, pp)
        full = expand(c2, state_lane % pp, (jnp.arange(nq * S5_LANES) % LANE) // hh, state_lane // pp)
        return full.reshape(nq, S5_LANES, gl * pp)

    bbr_t, bbi_t = bbr.transpose(0, 1, 3, 2), bbi.transpose(0, 1, 3, 2)

    def state_in(dd, sel):
        return _cmul(pwr[sel, dd][:, :, None, :], pwi[sel, dd][:, :, None, :], bbr_t[dd][None], bbi_t[dd][None])

    ein = jnp.concatenate([tile_in(c) for c in state_in(0, jnp.arange(tc - 1, -1, -1))]
                          + [tile_in(c) for c in state_in(1, jnp.arange(tc))], axis=2)

    def tile_out(coef):
        c2 = coef.transpose(1, 3, 0, 2).reshape(g * pp, tc * hh)
        full = expand(c2, (chunk_lane // LANE) * hh + chunk_lane % hh,
                      (jnp.arange(g * pp) % (gl * pp)) // pp, (chunk_lane % LANE) // hh)
        return full.reshape(nq, gl * pp, S5_LANES)

    ofr, ofi = c_times_pow(0, jnp.arange(1, tc + 1))
    obr, obi = c_times_pow(1, jnp.arange(tc, 0, -1))
    gout = jnp.concatenate([tile_out(ofr), tile_out(-ofi), tile_out(obr), tile_out(-obi)], axis=1)

    lam16 = jnp.stack([pwr[tc, 0], pwi[tc, 0], pwr[tc, 1], pwi[tc, 1]], axis=0)
    lam16 = lam16.reshape(4, nq, gl * pp).transpose(1, 0, 2)
    return dict(s5_m=m, s5_ein=ein, s5_gout=gout,
                s5_lam=lam16.astype(F32), s5_d=d.reshape(nq, 1, LANE).astype(F32))


def _fft_dims(l):
    n = 2 * l
    bits = n.bit_length() - 1
    assert n == 1 << bits and bits % 2 == 0, "sequence length must give a square power-of-two DFT size"
    n1 = 1 << (bits // 2)
    hp = -(-(n1 // 2 + 1) // SUBLANE) * SUBLANE
    return n1, n1, hp


def _t2_rows(t2):
    return pl.ds(pl.multiple_of(t2 * SUBLANE, SUBLANE), SUBLANE)


def _a_rows(t2hi, t2lo, hp):
    return pl.ds(t2hi * (hp * SUBLANE) + t2lo, hp, stride=SUBLANE)


def _a_tile(t2hi, k1, hp):
    return pl.ds(pl.multiple_of((t2hi * hp + k1) * SUBLANE, SUBLANE), SUBLANE)


def _a_load_k1(ref, k1, n2, hp):
    return jnp.concatenate([ref[_a_tile(t, k1, hp), :] for t in range(n2 // SUBLANE)], axis=0)


def _a_store_k1(ref, k1, val, n2, hp):
    for t in range(n2 // SUBLANE):
        ref[_a_tile(t, k1, hp), :] = val[t * SUBLANE:(t + 1) * SUBLANE, :]


def _k1_block(hp):
    return max(d for d in range(2, K1_BLOCK_MAX + 1, 2) if hp % d == 0)


def _dft_tables(l):
    n = 2 * l
    n1, n2, hp = _fft_dims(l)
    t2 = jnp.arange(n2, dtype=jnp.int32)[:, None, None]
    k1 = jnp.arange(hp, dtype=jnp.int32)[None, :, None]
    t1 = jnp.arange(n1, dtype=jnp.int32)[None, None, :]
    ang_a = (2.0 * math.pi / n1) * ((t1 * k1) % n1).astype(F32)
    ang_b = (2.0 * math.pi / n) * ((t2 * k1) % n).astype(F32)
    cos1 = jnp.cos(ang_a) * jnp.cos(ang_b) - jnp.sin(ang_a) * jnp.sin(ang_b)
    sin1 = jnp.sin(ang_a) * jnp.cos(ang_b) + jnp.cos(ang_a) * jnp.sin(ang_b)
    e1 = jnp.concatenate([cos1, -sin1], axis=1)
    wgt = jnp.where((k1 == 0) | (k1 == n1 // 2), 1.0, jnp.where(k1 < n1 // 2, 2.0, 0.0))
    half = n1 // 2
    einv = jnp.concatenate([(wgt * cos1)[:, :, :half], (-wgt * sin1)[:, :, :half]], axis=1)
    einv = einv.transpose(0, 2, 1)
    a = jnp.arange(n2, dtype=jnp.int32)
    ang2 = (2.0 * math.pi / n2) * ((a[:, None] * a[None, :]) % n2).astype(F32)
    c2, s2 = jnp.cos(ang2), jnp.sin(ang2)
    f2c = jnp.block([[c2, s2], [-s2, c2]])
    f2ic = jnp.block([[c2, -s2], [s2, c2]])
    return dict(e1=e1.astype(BF16), einv=einv.astype(BF16), f2c=f2c.astype(BF16), f2ic=f2ic.astype(BF16))


def _filter_kernel(w1_ref, b1_ref, w2_ref, b2_ref, sf_ref, om_ref, w3f_ref, w3b_ref, ldf_ref, ldb_ref,
                   e1_ref, f2c_ref, kr_ref, ki_ref, hdn_ref, are_ref, aim_ref, scale_ref, *, l, n1, n2, hp, k1b):
    j = pl.program_id(0)
    kb = pl.program_id(1)
    n = 2 * l

    def positions(t2):
        t1 = lax.broadcasted_iota(jnp.int32, (n1, LANE), 0)
        i = t1 * n2 + t2
        return i, jnp.where(i <= l, i, n - i).astype(F32)

    @pl.when((j == 0) & (kb == 0))
    def _():
        def body(t2, carry):
            _, pos = positions(t2)
            ang = pos * om_ref[...]
            lane = lax.broadcasted_iota(jnp.int32, (n1, LANE), 1)
            feats = jnp.where(lane == 0, pos / HYENA_TIME_SCALE,
                              jnp.where(lane <= HYENA_BANDS, jnp.sin(ang),
                                        jnp.where(lane <= 2 * HYENA_BANDS, jnp.cos(ang), 0.0)))
            h1 = jnp.sin(sf_ref[0:1, :] * (jnp.dot(feats, w1_ref[...], preferred_element_type=F32) + b1_ref[...]))
            h2 = jnp.sin(sf_ref[1:2, :] * (jnp.dot(h1, w2_ref[...], preferred_element_type=F32) + b2_ref[...]))
            hdn_ref[pl.ds(pl.multiple_of(t2 * n1, n1), n1), :] = h2
            return carry
        lax.fori_loop(0, n2, body, 0)

    @pl.when(kb == 0)
    def _():
        def body(t2hi, asum):
            for t2lo in range(SUBLANE):
                t2 = t2hi * SUBLANE + t2lo
                i, pos = positions(t2)
                h2 = hdn_ref[pl.ds(pl.multiple_of(t2 * n1, n1), n1), :].astype(BF16)
                t_lin = pos / HYENA_TIME_SCALE
                fwd = jnp.dot(h2, w3f_ref[...], preferred_element_type=F32) * jnp.exp(-t_lin * jnp.exp(ldf_ref[...]))
                bwd = jnp.dot(h2, w3b_ref[...], preferred_element_type=F32) * jnp.exp(-t_lin * jnp.exp(ldb_ref[...]))
                k = jnp.where(i < l, fwd, jnp.where(i > l, bwd, 0.0))
                r = jnp.dot(e1_ref[t2], k.astype(BF16), preferred_element_type=F32)
                are_ref[_a_rows(t2hi, t2lo, hp), :] = r[:hp]
                aim_ref[_a_rows(t2hi, t2lo, hp), :] = r[hp:]
                asum = asum + jnp.sum(jnp.abs(k), axis=0, keepdims=True)
            return asum
        asum = lax.fori_loop(0, n2 // SUBLANE, body, jnp.zeros((1, LANE), F32))
        scale_ref[...] = 1.0 / ((asum + FILTER_EPS) * n)

    for kk in range(0, k1b, 2):
        k1 = kb * k1b + kk
        pair = lambda ref: jnp.concatenate([_a_load_k1(ref, k1 + e, n2, hp) for e in range(2)], axis=1)
        rhs = jnp.concatenate([pair(are_ref), pair(aim_ref)], axis=0).astype(BF16)
        x = jnp.dot(f2c_ref[...], rhs, preferred_element_type=F32)
        for e in range(2):
            rows = slice((kk + e) * n2, (kk + e + 1) * n2)
            kr_ref[rows, :] = x[:n2, e * LANE:(e + 1) * LANE] * scale_ref[...]
            ki_ref[rows, :] = x[n2:, e * LANE:(e + 1) * LANE] * scale_ref[...]


def _filter_call(l, p, tb):
    n1, n2, hp = _fft_dims(l)
    k1b = _k1_block(hp)
    n = 2 * l
    nt = HYENA_ORDER * D_HYENA // LANE
    per = D_HYENA // LANE
    fwd_col = lambda j, kb: (0, (j // per) * 2 * per + j % per)
    bwd_col = lambda j, kb: (0, (j // per) * 2 * per + per + j % per)
    spec_out = pl.BlockSpec((k1b * n2, LANE), lambda j, kb: (kb, j))
    a_shape = (n2 * hp, LANE)
    return pl.pallas_call(
        functools.partial(_filter_kernel, l=l, n1=n1, n2=n2, hp=hp, k1b=k1b),
        out_shape=(jax.ShapeDtypeStruct((hp * n2, nt * LANE), F32),) * 2,
        grid=(nt, hp // k1b),
        in_specs=[_resident((LANE, LANE)), _resident((1, LANE)), _resident((LANE, LANE)), _resident((1, LANE)),
                  _resident((2, LANE)), _resident((1, LANE)),
                  pl.BlockSpec((LANE, LANE), fwd_col), pl.BlockSpec((LANE, LANE), bwd_col),
                  pl.BlockSpec((1, LANE), fwd_col), pl.BlockSpec((1, LANE), bwd_col),
                  _resident((n2, 2 * hp, n1)), _resident((2 * n2, 2 * n2))],
        out_specs=(spec_out, spec_out),
        scratch_shapes=[pltpu.VMEM((n, LANE), F32), pltpu.VMEM(a_shape, F32),
                        pltpu.VMEM(a_shape, F32), pltpu.VMEM((1, LANE), F32)],
        compiler_params=_params(("arbitrary", "arbitrary")),
        name="hyena_filter_spectrum",
    )(p["filt_w1"], p["filt_b1"], p["filt_w2"], p["filt_b2"], p["filt_sf"], p["filt_om"],
      p["filt_w3"], p["filt_w3"], p["filt_ld"], p["filt_ld"], tb["e1"], tb["f2c"])


def _short_conv_column(ref, w_ref, b_ref, t2hi, t2lo, n2, half):
    t2 = t2hi * SUBLANE + t2lo
    col = lambda c: ref[:, _t2_rows(c), :].reshape(half, LANE)
    t1 = lax.broadcasted_iota(jnp.int32, (half, LANE), 0)
    before = col(jnp.maximum(t2 - 1, 0))
    if t2lo == 0:
        wrapped = jnp.where(t1 == 0, 0.0, pltpu.roll(col(n2 - 1), 1, 0))
        before = jnp.where(t2hi == 0, wrapped, before)
    after = col(jnp.minimum(t2 + 1, n2 - 1))
    if t2lo == SUBLANE - 1:
        wrapped = jnp.where(t1 == half - 1, 0.0, pltpu.roll(col(0), half - 1, 0))
        after = jnp.where(t2hi == n2 // SUBLANE - 1, wrapped, after)
    return ((b_ref[...] + before * w_ref[0:1, :]) + col(t2) * w_ref[1:2, :]) + after * w_ref[2:3, :]


def _conv_kernel(z_ref, g_ref, bias_ref, zw_ref, zb_ref, gw_ref, gb_ref, e1_ref, einv_ref, f2c_ref, f2ic_ref,
                 kr_ref, ki_ref, o_ref, are_ref, aim_ref, zc_ref, *, n1, n2, hp, k1b, conv_z):
    s = pl.program_id(2)
    half = n1 // 2
    tiles = (half // SUBLANE, SUBLANE, LANE)

    @pl.when(s == 0)
    def _():
        def body(t2hi, carry):
            for t2lo in range(SUBLANE):
                t2 = t2hi * SUBLANE + t2lo
                if conv_z:
                    z = _short_conv_column(z_ref, zw_ref, zb_ref, t2hi, t2lo, n2, half)
                    zc_ref[:, _t2_rows(t2), :] = z.reshape(tiles)
                else:
                    z = z_ref[:, _t2_rows(t2), :].reshape(half, LANE)
                z = z.astype(BF16)
                r = jnp.dot(e1_ref[t2, :, 0:half], z, preferred_element_type=F32)
                are_ref[_a_rows(t2hi, t2lo, hp), :] = r[:hp]
                aim_ref[_a_rows(t2hi, t2lo, hp), :] = r[hp:]
            return carry
        lax.fori_loop(0, n2 // SUBLANE, body, 0, unroll=2)

    for kk in range(0, k1b, 2):
        k1 = s * k1b + kk
        pair = lambda ref: jnp.concatenate([_a_load_k1(ref, k1 + e, n2, hp) for e in range(2)], axis=1)
        rhs = jnp.concatenate([pair(are_ref), pair(aim_ref)], axis=0).astype(BF16)
        x = jnp.dot(f2c_ref[...], rhs, preferred_element_type=F32)
        xr, xi = x[:n2], x[n2:]
        spec = lambda ref: jnp.concatenate([ref[(kk + e) * n2:(kk + e + 1) * n2, :] for e in range(2)], axis=1)
        fr, fi = spec(kr_ref), spec(ki_ref)
        y = jnp.concatenate([xr * fr - xi * fi, xr * fi + xi * fr], axis=0).astype(BF16)
        back = jnp.dot(f2ic_ref[...], y, preferred_element_type=F32)
        for e in range(2):
            _a_store_k1(are_ref, k1 + e, back[:n2, e * LANE:(e + 1) * LANE], n2, hp)
            _a_store_k1(aim_ref, k1 + e, back[n2:, e * LANE:(e + 1) * LANE], n2, hp)

    @pl.when(s == pl.num_programs(2) - 1)
    def _():
        def body(t2hi, carry):
            for t2lo in range(SUBLANE):
                t2 = t2hi * SUBLANE + t2lo
                rows = _a_rows(t2hi, t2lo, hp)
                rhs = jnp.concatenate([are_ref[rows, :], aim_ref[rows, :]], axis=0).astype(BF16)
                y = jnp.dot(einv_ref[t2], rhs, preferred_element_type=F32)
                gate = _short_conv_column(g_ref, gw_ref, gb_ref, t2hi, t2lo, n2, half)
                z = (zc_ref if conv_z else z_ref)[:, _t2_rows(t2), :].reshape(half, LANE)
                o_ref[:, _t2_rows(t2), :] = (gate * (y + bias_ref[...] * z)).reshape(tiles)
            return carry
        lax.fori_loop(0, n2 // SUBLANE, body, 0, unroll=2)


def _conv_call(z, z_col0, g, g_col0, p, kr, ki, order, tb, conv_z):
    b, _, t1_tiles, tt, _ = z.shape
    n2, half = tt // SUBLANE, t1_tiles * SUBLANE
    n1, _, hp = _fft_dims(n2 * half)
    k1b = _k1_block(hp)
    per = D_HYENA // LANE
    seq = lambda col0: pl.BlockSpec((None, None, t1_tiles, tt, LANE), lambda c, i, s: (i, col0 + c, 0, 0, 0))
    taps = lambda col0: pl.BlockSpec((3, LANE), lambda c, i, s: (0, col0 + c))
    shift = lambda col0: pl.BlockSpec((1, LANE), lambda c, i, s: (0, col0 + c))
    spec_k = pl.BlockSpec((k1b * n2, LANE), lambda c, i, s: (s, order * per + c))
    a_shape = (n2 * hp, LANE)
    w_col0 = z_col0 if conv_z else g_col0
    return pl.pallas_call(
        functools.partial(_conv_kernel, n1=n1, n2=n2, hp=hp, k1b=k1b, conv_z=conv_z),
        out_shape=jax.ShapeDtypeStruct((b, per, t1_tiles, tt, LANE), F32),
        grid=(per, b, hp // k1b),
        in_specs=[seq(z_col0), seq(g_col0),
                  pl.BlockSpec((1, LANE), lambda c, i, s: (0, order * per + c)),
                  taps(w_col0), shift(w_col0), taps(g_col0), shift(g_col0),
                  _resident((n2, 2 * hp, n1)), _resident((n2, n1 // 2, 2 * hp)),
                  _resident((2 * n2, 2 * n2)), _resident((2 * n2, 2 * n2)),
                  spec_k, spec_k],
        out_specs=pl.BlockSpec((None, None, t1_tiles, tt, LANE), lambda c, i, s: (i, c, 0, 0, 0)),
        scratch_shapes=[pltpu.VMEM(a_shape, F32), pltpu.VMEM(a_shape, F32),
                        pltpu.VMEM((t1_tiles, tt, LANE) if conv_z else (SUBLANE, LANE), F32)],
        compiler_params=_params(("parallel", "parallel", "arbitrary")),
        name=f"hyena_conv_order{order}",
    )(z, g, p["hy_bias"], p["short_w"], p["short_b"], p["short_w"], p["short_b"],
      tb["e1"], tb["einv"], tb["f2c"], tb["f2ic"], kr, ki)


def _prepare(ffn1_w_gate, ffn1_w_up, ffn1_w_down, ln1_g, ln1_b, w_in,
             ssm_lam_re, ssm_lam_im, ssm_log_step, ssm_b_re, ssm_b_im, ssm_c_re, ssm_c_im,
             ssm_d, ssm_glu_w, ssm_glu_b, ssm_norm_g,
             hy_short_w, hy_short_b, hy_filt_w1, hy_filt_b1, hy_filt_w2, hy_filt_b2, hy_filt_w3,
             hy_sin_freq, hy_log_decay, hy_bias, hy_norm_g, w_out, ln2_g, ln2_b,
             ffn2_w_gate, ffn2_w_up, ffn2_w_down, ln3_g, ln3_b):
    row = lambda a: a[0].reshape(1, -1).astype(F32)
    hid = HYENA_FILTER_HIDDEN
    pad2 = lambda a, r, c: jnp.zeros((r, c), F32).at[:a.shape[0], :a.shape[1]].set(a.astype(F32))
    omega = jnp.exp(-math.log(HYENA_MAX_PERIOD) * jnp.arange(HYENA_BANDS, dtype=F32) / HYENA_BANDS)
    om = jnp.zeros((1, LANE), F32).at[0, 1:1 + HYENA_BANDS].set(omega).at[0, 1 + HYENA_BANDS:HYENA_POS_DIM].set(omega)
    p = dict(
        ffn1_wg=ffn1_w_gate[0].astype(BF16), ffn1_wu=ffn1_w_up[0].astype(BF16), ffn1_wd=ffn1_w_down[0].astype(BF16),
        ln1_g=row(ln1_g), ln1_b=row(ln1_b), w_in=w_in[0].astype(BF16),
        glu_w=ssm_glu_w[0].astype(BF16), glu_b=row(ssm_glu_b), ssm_norm_g=row(ssm_norm_g),
        hy_norm_g=row(hy_norm_g), w_out=w_out[0].astype(BF16), ln2_g=row(ln2_g), ln2_b=row(ln2_b),
        ffn2_wg=ffn2_w_gate[0].astype(BF16), ffn2_wu=ffn2_w_up[0].astype(BF16), ffn2_wd=ffn2_w_down[0].astype(BF16),
        ln3_g=row(ln3_g), ln3_b=row(ln3_b),
        short_w=hy_short_w[0].astype(F32), short_b=row(hy_short_b),
        filt_w1=pad2(hy_filt_w1[0], LANE, LANE), filt_b1=pad2(hy_filt_b1[0].reshape(1, hid), 1, LANE),
        filt_w2=pad2(hy_filt_w2[0], LANE, LANE), filt_b2=pad2(hy_filt_b2[0].reshape(1, hid), 1, LANE),
        filt_sf=pad2(hy_sin_freq[0], 2, LANE), filt_om=om,
        filt_w3=pad2(hy_filt_w3[0], LANE, hy_filt_w3.shape[-1]).astype(BF16), filt_ld=row(hy_log_decay),
        hy_bias=hy_bias[0].reshape(1, HYENA_ORDER * D_HYENA).astype(F32),
    )
    f32 = lambda a: a[0].astype(F32)
    p.update(_s5_tables(f32(ssm_lam_re), f32(ssm_lam_im), f32(ssm_log_step), f32(ssm_b_re), f32(ssm_b_im),
                        f32(ssm_c_re), f32(ssm_c_im), f32(ssm_d)))
    return p


def _trunk(x, p):
    b, l, d = x.shape
    per = D_HYENA // LANE
    h1, u, hy_raw = _ffn1_call(x, p)
    y_ssm = _s5_call(u.reshape(b, l, D_SSM), p)
    tb = _dft_tables(l)
    kr, ki = _filter_call(l, p, tb)
    z1 = _conv_call(hy_raw, 0, hy_raw, per, p, kr, ki, 0, tb, True)
    z2 = _conv_call(z1, 0, hy_raw, 2 * per, p, kr, ki, 1, tb, False)
    return _final_call(y_ssm, z2, h1.reshape(b, l, d), p)


def kernel(x_prompt, x_sample, ffn1_w_gate, ffn1_w_up, ffn1_w_down, ln1_g, ln1_b, w_in, ssm_lam_re, ssm_lam_im, ssm_log_step, ssm_b_re, ssm_b_im, ssm_c_re, ssm_c_im, ssm_d, ssm_glu_w, ssm_glu_b, ssm_norm_g, hy_short_w, hy_short_b, hy_filt_w1, hy_filt_b1, hy_filt_w2, hy_filt_b2, hy_filt_w3, hy_sin_freq, hy_log_decay, hy_bias, hy_norm_g, w_out, ln2_g, ln2_b, ffn2_w_gate, ffn2_w_up, ffn2_w_down, ln3_g, ln3_b):
    p = _prepare(ffn1_w_gate, ffn1_w_up, ffn1_w_down, ln1_g, ln1_b, w_in,
                 ssm_lam_re, ssm_lam_im, ssm_log_step, ssm_b_re, ssm_b_im, ssm_c_re, ssm_c_im,
                 ssm_d, ssm_glu_w, ssm_glu_b, ssm_norm_g,
                 hy_short_w, hy_short_b, hy_filt_w1, hy_filt_b1, hy_filt_w2, hy_filt_b2, hy_filt_w3,
                 hy_sin_freq, hy_log_decay, hy_bias, hy_norm_g, w_out, ln2_g, ln2_b,
                 ffn2_w_gate, ffn2_w_up, ffn2_w_down, ln3_g, ln3_b)
    return (_trunk(x_prompt, p), _trunk(x_sample, p))
```

```python
import functools
import math

import jax
import jax.numpy as jnp
from jax import lax
from jax.experimental import pallas as pl
from jax.experimental.pallas import tpu as pltpu

F32 = jnp.float32
BF16 = jnp.bfloat16

D_MODEL = 1024
D_SSM = 512
SSM_GROUP = 16
N_SSM_GROUPS = D_SSM // SSM_GROUP
SSM_STATE = 64
D_HYENA = 512
HYENA_ORDER = 2
HYENA_BANDS = 8
HYENA_POS_DIM = 1 + 2 * HYENA_BANDS
HYENA_FILTER_HIDDEN = 64
HYENA_TIME_SCALE = 4096.0
HYENA_MAX_PERIOD = 10000.0
D_HY_IN = (HYENA_ORDER + 1) * D_HYENA
D_IN = D_SSM + D_HY_IN
D_FF = 128 * math.ceil(8 * D_MODEL / 3 / 128)
LN_EPS = 1e-5
RMS_EPS = 1e-6
FILTER_EPS = 1e-6
DEPTH = 1
DEEPNORM_ALPHA = (2.0 * DEPTH) ** 0.25

LANE = 128
SUBLANE = 8
V7X_VMEM_BYTES = 64 * 1024 * 1024
VMEM_LIMIT_BYTES = V7X_VMEM_BYTES - 8 * 1024 * 1024

TOKEN_TILE = 512
FF_CHUNK = 256
S5_CHUNK = 16
S5_LANES = S5_CHUNK * LANE
S5_STATE_LANES = (LANE // SSM_GROUP) * SSM_STATE
K1_BLOCK_MAX = 24


def _params(semantics):
    return pltpu.CompilerParams(dimension_semantics=semantics, vmem_limit_bytes=VMEM_LIMIT_BYTES)


def _resident(shape):
    nd = len(shape)
    return pl.BlockSpec(shape, lambda *_: (0,) * nd, pipeline_mode=pl.Buffered(1))


def _layer_norm(x, g, b):
    mu = jnp.mean(x, axis=-1, keepdims=True)
    xc = x - mu
    var = jnp.mean(xc * xc, axis=-1, keepdims=True)
    return xc * lax.rsqrt(var + LN_EPS) * g + b


def _rms_norm(x, g):
    ms = jnp.mean(x * x, axis=-1, keepdims=True)
    return x * lax.rsqrt(ms + RMS_EPS) * g


def _ffn_ln(x, wg_ref, wu_ref, wd_ref, g_ref, b_ref, acc_ref):
    xb = x.astype(BF16)
    acc_ref[...] = jnp.zeros_like(acc_ref)

    def body(c, carry):
        off = pl.multiple_of(c * FF_CHUNK, FF_CHUNK)
        gate = jnp.dot(xb, wg_ref[:, pl.ds(off, FF_CHUNK)], preferred_element_type=F32)
        up = jnp.dot(xb, wu_ref[:, pl.ds(off, FF_CHUNK)], preferred_element_type=F32)
        act = (jax.nn.silu(gate) * up).astype(BF16)
        acc_ref[...] += jnp.dot(act, wd_ref[pl.ds(off, FF_CHUNK), :], preferred_element_type=F32)
        return carry

    lax.fori_loop(0, D_FF // FF_CHUNK, body, 0, unroll=True)
    return _layer_norm(DEEPNORM_ALPHA * x + 0.5 * acc_ref[...], g_ref[...], b_ref[...])


def _ffn1_kernel(x_ref, wg_ref, wu_ref, wd_ref, g_ref, b_ref, win_ref, h_ref, u_ref, hy_ref, acc_ref, *, n2):
    h = _ffn_ln(x_ref[...], wg_ref, wu_ref, wd_ref, g_ref, b_ref, acc_ref)
    h_ref[...] = h
    hb = h.astype(BF16)
    u_ref[...] = jnp.dot(hb, win_ref[:, :D_SSM], preferred_element_type=F32)
    hy = jnp.dot(hb, win_ref[:, D_SSM:], preferred_element_type=F32)
    per = TOKEN_TILE // n2
    first = (pl.program_id(0) % (SUBLANE // per)) * per
    for ct in range(D_HY_IN // LANE):
        for t1 in range(per):
            hy_ref[ct, pl.ds(first + t1, n2, stride=SUBLANE), :] = hy[t1 * n2:(t1 + 1) * n2, ct * LANE:(ct + 1) * LANE]


def _ffn1_call(x, p):
    b, l, _ = x.shape
    _, n2, _ = _fft_dims(l)
    tm = TOKEN_TILE
    assert l % tm == 0 and tm % n2 == 0 and SUBLANE % (tm // n2) == 0
    t = b * l
    steps = SUBLANE * n2 // tm
    tiles = l // tm
    row = lambda w: pl.BlockSpec((tm, w), lambda i: (i, 0))
    return pl.pallas_call(
        functools.partial(_ffn1_kernel, n2=n2),
        out_shape=(jax.ShapeDtypeStruct((t, D_MODEL), F32),
                   jax.ShapeDtypeStruct((t, D_SSM), F32),
                   jax.ShapeDtypeStruct((b, D_HY_IN // LANE, l // (SUBLANE * n2), SUBLANE * n2, LANE), F32)),
        grid=(t // tm,),
        in_specs=[row(D_MODEL), _resident((D_MODEL, D_FF)), _resident((D_MODEL, D_FF)),
                  _resident((D_FF, D_MODEL)), _resident((1, D_MODEL)), _resident((1, D_MODEL)),
                  _resident((D_MODEL, D_IN))],
        out_specs=(row(D_MODEL), row(D_SSM),
                   pl.BlockSpec((None, D_HY_IN // LANE, None, SUBLANE * n2, LANE),
                                lambda i: (i // tiles, 0, (i % tiles) // steps, 0, 0))),
        scratch_shapes=[pltpu.VMEM((tm, D_MODEL), F32)],
        compiler_params=_params(("arbitrary",)),
        name="ffn1_ln1_proj",
    )(x.reshape(t, D_MODEL), p["ffn1_wg"], p["ffn1_wu"], p["ffn1_wd"], p["ln1_g"], p["ln1_b"], p["w_in"])


def _final_kernel(ys_ref, yh_ref, h_ref, gw_ref, gb_ref, sg_ref, hg_ref, wo_ref, l2g_ref, l2b_ref,
                  wg_ref, wu_ref, wd_ref, l3g_ref, l3b_ref, o_ref, acc_ref, *, n2):
    per = TOKEN_TILE // n2

    def sub_tile(hh, carry):
        rows = pl.ds(pl.multiple_of(hh * TOKEN_TILE, TOKEN_TILE), TOKEN_TILE)
        y_hy = jnp.concatenate(
            [jnp.concatenate([yh_ref[ct, pl.ds(hh * per + jj, n2, stride=SUBLANE), :]
                              for ct in range(D_HYENA // LANE)], axis=1) for jj in range(per)], axis=0)
        g = jax.nn.gelu(ys_ref[rows, :])
        gate = jax.nn.sigmoid(jnp.dot(g.astype(BF16), gw_ref[...], preferred_element_type=F32) + gb_ref[...])
        y_ssm = _rms_norm(g * gate, sg_ref[...])
        y_hy = _rms_norm(y_hy, hg_ref[...])
        mix = (jnp.dot(y_ssm.astype(BF16), wo_ref[:D_SSM, :], preferred_element_type=F32)
               + jnp.dot(y_hy.astype(BF16), wo_ref[D_SSM:, :], preferred_element_type=F32))
        x2 = _layer_norm(DEEPNORM_ALPHA * h_ref[rows, :] + mix, l2g_ref[...], l2b_ref[...])
        o_ref[rows, :] = _ffn_ln(x2, wg_ref, wu_ref, wd_ref, l3g_ref, l3b_ref, acc_ref)
        return carry

    lax.fori_loop(0, SUBLANE * n2 // TOKEN_TILE, sub_tile, 0)


def _final_call(y_ssm, y_hy, h1, p):
    b, l, _ = y_ssm.shape
    tt = y_hy.shape[3]
    n2 = tt // SUBLANE
    assert tt % TOKEN_TILE == 0 and TOKEN_TILE % n2 == 0
    row = lambda w: pl.BlockSpec((None, tt, w), lambda i, a: (i, a, 0))
    return pl.pallas_call(
        functools.partial(_final_kernel, n2=n2),
        out_shape=jax.ShapeDtypeStruct((b, l, D_MODEL), F32),
        grid=(b, l // tt),
        in_specs=[row(D_SSM),
                  pl.BlockSpec((None, D_HYENA // LANE, None, tt, LANE), lambda i, a: (i, 0, a, 0, 0)),
                  row(D_MODEL),
                  _resident((D_SSM, D_SSM)), _resident((1, D_SSM)), _resident((1, D_SSM)),
                  _resident((1, D_HYENA)), _resident((D_MODEL, D_MODEL)),
                  _resident((1, D_MODEL)), _resident((1, D_MODEL)),
                  _resident((D_MODEL, D_FF)), _resident((D_MODEL, D_FF)), _resident((D_FF, D_MODEL)),
                  _resident((1, D_MODEL)), _resident((1, D_MODEL))],
        out_specs=row(D_MODEL),
        scratch_shapes=[pltpu.VMEM((TOKEN_TILE, D_MODEL), F32)],
        compiler_params=_params(("parallel", "parallel")),
        name="mix_ln2_ffn2_ln3",
    )(y_ssm, y_hy, h1, p["glu_w"], p["glu_b"], p["ssm_norm_g"], p["hy_norm_g"], p["w_out"],
      p["ln2_g"], p["ln2_b"], p["ffn2_wg"], p["ffn2_wu"], p["ffn2_wd"], p["ln3_g"], p["ln3_b"])


def _s5_kernel(u_ref, blk_ref, ein_ref, gout_ref, lam_ref, d_ref, y_ref, ucat_ref, st_ref, m_ref, *, nch, rb):
    sl = S5_STATE_LANES

    @pl.when(pl.program_id(1) == 0)
    def _():
        for t_in in range(S5_CHUNK):
            for t_out in range(S5_CHUNK):
                m_ref[t_in * LANE:(t_in + 1) * LANE, t_out * LANE:(t_out + 1) * LANE] = (
                    blk_ref[t_out - t_in + S5_CHUNK - 1])

    for t in range(S5_CHUNK):
        ucat_ref[:, t * LANE:(t + 1) * LANE] = u_ref[pl.ds(t, nch, stride=S5_CHUNK), :].astype(BF16)

    def state_in(i, carry):
        r0 = pl.multiple_of(i * rb, rb)
        lhs = ucat_ref[pl.ds(r0, rb), :]
        for cb in range(4):
            st_ref[pl.ds(r0, rb), cb * sl:(cb + 1) * sl] = jnp.dot(
                lhs, ein_ref[:, cb * sl:(cb + 1) * sl], preferred_element_type=F32)
        return carry

    lax.fori_loop(0, nch // rb, state_in, 0)

    lfr, lfi = lam_ref[0:1, :], lam_ref[1:2, :]
    lbr, lbi = lam_ref[2:3, :], lam_ref[3:4, :]

    def scan(i, carry):
        sfr, sfi, sbr, sbi = carry
        j = nch - 1 - i
        xfr = st_ref[pl.ds(i, 1), 0:sl]
        xfi = st_ref[pl.ds(i, 1), sl:2 * sl]
        st_ref[pl.ds(i, 1), 0:sl] = sfr
        st_ref[pl.ds(i, 1), sl:2 * sl] = sfi
        xbr = st_ref[pl.ds(j, 1), 2 * sl:3 * sl]
        xbi = st_ref[pl.ds(j, 1), 3 * sl:4 * sl]
        st_ref[pl.ds(j, 1), 2 * sl:3 * sl] = sbr
        st_ref[pl.ds(j, 1), 3 * sl:4 * sl] = sbi
        return (lfr * sfr - lfi * sfi + xfr, lfr * sfi + lfi * sfr + xfi,
                lbr * sbr - lbi * sbi + xbr, lbr * sbi + lbi * sbr + xbi)

    zero = jnp.zeros((1, sl), F32)
    lax.fori_loop(0, nch, scan, (zero, zero, zero, zero))

    def emit(i, carry):
        r0 = pl.multiple_of(i * rb, rb)
        lhs_u = ucat_ref[pl.ds(r0, rb), :]
        lhs_s = st_ref[pl.ds(r0, rb), :].astype(BF16)
        for cb in range(4):
            cols = slice(cb * 4 * LANE, (cb + 1) * 4 * LANE)
            yc = (jnp.dot(lhs_u, m_ref[:, cols], preferred_element_type=F32)
                  + jnp.dot(lhs_s, gout_ref[:, cols], preferred_element_type=F32))
            for tt in range(4):
                rows = pl.ds(r0 * S5_CHUNK + cb * 4 + tt, rb, stride=S5_CHUNK)
                y_ref[rows, :] = yc[:, tt * LANE:(tt + 1) * LANE] + d_ref[...] * u_ref[rows, :]
        return carry

    lax.fori_loop(0, nch // rb, emit, 0)


def _s5_call(u, p):
    b, l, _ = u.shape
    nch = l // S5_CHUNK
    rb = min(nch, 256)
    nq = D_SSM // LANE
    mat = pl.BlockSpec((None, S5_LANES, S5_LANES), lambda q, i: (q, 0, 0), pipeline_mode=pl.Buffered(1))
    lags = 2 * S5_CHUNK - 1
    return pl.pallas_call(
        functools.partial(_s5_kernel, nch=nch, rb=rb),
        out_shape=jax.ShapeDtypeStruct((b, l, D_SSM), F32),
        grid=(nq, b),
        in_specs=[pl.BlockSpec((None, l, LANE), lambda q, i: (i, 0, q)),
                  pl.BlockSpec((None, lags, LANE, LANE), lambda q, i: (q, 0, 0, 0), pipeline_mode=pl.Buffered(1)),
                  mat, mat,
                  pl.BlockSpec((None, 4, S5_STATE_LANES), lambda q, i: (q, 0, 0)),
                  pl.BlockSpec((None, 1, LANE), lambda q, i: (q, 0, 0))],
        out_specs=pl.BlockSpec((None, l, LANE), lambda q, i: (i, 0, q)),
        scratch_shapes=[pltpu.VMEM((nch, S5_LANES), BF16), pltpu.VMEM((nch, 4 * S5_STATE_LANES), F32),
                        pltpu.VMEM((S5_LANES, S5_LANES), BF16)],
        compiler_params=_params(("arbitrary", "arbitrary")),
        name="s5_chunked",
    )(u, p["s5_blk"], p["s5_ein"], p["s5_gout"], p["s5_lam"], p["s5_d"])


def _cmul(ar, ai, br, bi):
    return ar * br - ai * bi, ar * bi + ai * br


def _s5_tables(lam_re, lam_im, log_step, b_re, b_im, c_re, c_im, d):
    g, hh, pp, tc = N_SSM_GROUPS, SSM_GROUP, SSM_STATE, S5_CHUNK
    gl = LANE // hh
    nq = g // gl
    step = jnp.exp(log_step)[..., None]
    mag = jnp.exp(lam_re * step)
    ar = mag * jnp.cos(lam_im * step)
    ai = mag * jnp.sin(lam_im * step)
    nr, ni = ar - 1.0, ai
    den = lam_re * lam_re + lam_im * lam_im
    qr = (nr * lam_re + ni * lam_im) / den
    qi = (ni * lam_re - nr * lam_im) / den
    bbr = qr[..., None] * b_re - qi[..., None] * b_im
    bbi = qr[..., None] * b_im + qi[..., None] * b_re
    pr, pi = [jnp.ones_like(ar)], [jnp.zeros_like(ai)]
    for _ in range(tc):
        r, i = _cmul(pr[-1], pi[-1], ar, ai)
        pr.append(r)
        pi.append(i)
    pwr, pwi = jnp.stack(pr), jnp.stack(pi)

    def c_times_pow(dd, sel):
        return _cmul(c_re[dd][None], c_im[dd][None], pwr[sel, dd][:, :, None, :], pwi[sel, dd][:, :, None, :])

    def lag_kernels(dd):
        zr, zi = (jnp.moveaxis(z, 3, 0)[..., None] for z in c_times_pow(dd, jnp.arange(tc)))
        br, bi = (jnp.moveaxis(bb[dd], 1, 0)[:, None, :, None, :] for bb in (bbr, bbi))
        return jnp.sum(zr * br - zi * bi, axis=0)

    kf, kb = lag_kernels(0), lag_kernels(1)
    kall = jnp.concatenate([kb[jnp.arange(tc - 1, 0, -1)], kf[:1] + kb[:1], kf[1:]], axis=0)

    def expand(compact, src_col, row_grp, col_grp):
        rep = (jnp.arange(compact.shape[1])[:, None] == src_col[None, :]).astype(BF16)
        full = jnp.dot(compact.astype(BF16), rep, preferred_element_type=BF16)
        return jnp.where(row_grp[:, None] == col_grp[None, :], full, jnp.zeros((), BF16))

    lane = jnp.arange(LANE)
    chunk_lane = jnp.arange(S5_LANES)

    k2 = kall.transpose(0, 1, 3, 2).reshape((2 * tc - 1) * g * hh, hh)
    blk = expand(k2, lane % hh, (jnp.arange(k2.shape[0]) % LANE) // hh, lane // hh)
    blk = blk.reshape(2 * tc - 1, nq, LANE, LANE).transpose(1, 0, 2, 3)

    bbr_t, bbi_t = bbr.transpose(0, 1, 3, 2), bbi.transpose(0, 1, 3, 2)

    def state_in(dd, sel):
        return _cmul(pwr[sel, dd][:, :, None, :], pwi[sel, dd][:, :, None, :], bbr_t[dd][None], bbi_t[dd][None])

    cin = jnp.concatenate([c.reshape(tc, nq, LANE, pp)
                           for c in state_in(0, jnp.arange(tc - 1, -1, -1))
                           + state_in(1, jnp.arange(tc))], axis=3)
    cin = cin.transpose(1, 0, 2, 3).reshape(nq * S5_LANES, 4 * pp)
    state_col = jnp.arange(4 * gl * pp)
    ein = expand(cin, (state_col // (gl * pp)) * pp + state_col % pp,
                 (jnp.arange(nq * S5_LANES) % LANE) // hh, (state_col % (gl * pp)) // pp)
    ein = ein.reshape(nq, S5_LANES, 4 * gl * pp)

    ofr, ofi = c_times_pow(0, jnp.arange(1, tc + 1))
    obr, obi = c_times_pow(1, jnp.arange(tc, 0, -1))
    cout = jnp.stack([c.transpose(1, 3, 0, 2).reshape(nq, gl * pp, tc * hh) for c in (ofr, -ofi, obr, -obi)], axis=1)
    cout = cout.reshape(nq * 4 * gl * pp, tc * hh)
    gout = expand(cout, (chunk_lane // LANE) * hh + chunk_lane % hh,
                  (jnp.arange(cout.shape[0]) % (gl * pp)) // pp, (chunk_lane % LANE) // hh)
    gout = gout.reshape(nq, 4 * gl * pp, S5_LANES)

    lam16 = jnp.stack([pwr[tc, 0], pwi[tc, 0], pwr[tc, 1], pwi[tc, 1]], axis=0)
    lam16 = lam16.reshape(4, nq, gl * pp).transpose(1, 0, 2)
    return dict(s5_blk=blk, s5_ein=ein, s5_gout=gout,
                s5_lam=lam16.astype(F32), s5_d=d.reshape(nq, 1, LANE).astype(F32))


def _fft_dims(l):
    n = 2 * l
    bits = n.bit_length() - 1
    assert n == 1 << bits and bits % 2 == 0, "sequence length must give a square power-of-two DFT size"
    n1 = 1 << (bits // 2)
    hp = -(-(n1 // 2 + 1) // SUBLANE) * SUBLANE
    return n1, n1, hp


def _t2_rows(t2):
    return pl.ds(pl.multiple_of(t2 * SUBLANE, SUBLANE), SUBLANE)


def _a_rows(t2hi, t2lo, hp):
    return pl.ds(t2hi * (hp * SUBLANE) + t2lo, hp, stride=SUBLANE)


def _a_tile(t2hi, k1, hp):
    return pl.ds(pl.multiple_of((t2hi * hp + k1) * SUBLANE, SUBLANE), SUBLANE)


def _a_load_k1(ref, k1, n2, hp):
    return jnp.concatenate([ref[_a_tile(t, k1, hp), :] for t in range(n2 // SUBLANE)], axis=0)


def _a_store_k1(ref, k1, val, n2, hp):
    for t in range(n2 // SUBLANE):
        ref[_a_tile(t, k1, hp), :] = val[t * SUBLANE:(t + 1) * SUBLANE, :]


def _k1_block(hp):
    return max(d for d in range(2, K1_BLOCK_MAX + 1, 2) if hp % d == 0)


def _dft_tables(l):
    n = 2 * l
    n1, n2, hp = _fft_dims(l)
    t2 = jnp.arange(n2, dtype=jnp.int32)[:, None, None]
    k1 = jnp.arange(hp, dtype=jnp.int32)[None, :, None]
    t1 = jnp.arange(n1, dtype=jnp.int32)[None, None, :]
    ang_a = (2.0 * math.pi / n1) * ((t1 * k1) % n1).astype(F32)
    ang_b = (2.0 * math.pi / n) * ((t2 * k1) % n).astype(F32)
    cos1 = jnp.cos(ang_a) * jnp.cos(ang_b) - jnp.sin(ang_a) * jnp.sin(ang_b)
    sin1 = jnp.sin(ang_a) * jnp.cos(ang_b) + jnp.cos(ang_a) * jnp.sin(ang_b)
    e1 = jnp.concatenate([cos1, -sin1], axis=1)
    wgt = jnp.where((k1 == 0) | (k1 == n1 // 2), 1.0, jnp.where(k1 < n1 // 2, 2.0, 0.0))
    half = n1 // 2
    einv = jnp.concatenate([(wgt * cos1)[:, :, :half], (-wgt * sin1)[:, :, :half]], axis=1)
    einv = einv.transpose(0, 2, 1)
    a = jnp.arange(n2, dtype=jnp.int32)
    ang2 = (2.0 * math.pi / n2) * ((a[:, None] * a[None, :]) % n2).astype(F32)
    c2, s2 = jnp.cos(ang2), jnp.sin(ang2)
    f2c = jnp.block([[c2, s2], [-s2, c2]])
    f2ic = jnp.block([[c2, -s2], [s2, c2]])
    return dict(e1=e1.astype(BF16), einv=einv.astype(BF16), f2c=f2c.astype(BF16), f2ic=f2ic.astype(BF16))


def _filter_kernel(w1_ref, b1_ref, w2_ref, b2_ref, sf_ref, om_ref, w3f_ref, w3b_ref, ldf_ref, ldb_ref,
                   e1_ref, f2c_ref, kr_ref, ki_ref, hdn_ref, are_ref, aim_ref, scale_ref, *, l, n1, n2, hp, k1b):
    j = pl.program_id(0)
    kb = pl.program_id(1)
    n = 2 * l

    def positions(t2):
        t1 = lax.broadcasted_iota(jnp.int32, (n1, LANE), 0)
        i = t1 * n2 + t2
        return i, jnp.where(i <= l, i, n - i).astype(F32)

    @pl.when((j == 0) & (kb == 0))
    def _():
        def body(t2, carry):
            _, pos = positions(t2)
            ang = pos * om_ref[...]
            lane = lax.broadcasted_iota(jnp.int32, (n1, LANE), 1)
            feats = jnp.where(lane == 0, pos / HYENA_TIME_SCALE,
                              jnp.where(lane <= HYENA_BANDS, jnp.sin(ang),
                                        jnp.where(lane <= 2 * HYENA_BANDS, jnp.cos(ang), 0.0)))
            h1 = jnp.sin(sf_ref[0:1, :] * (jnp.dot(feats, w1_ref[...], preferred_element_type=F32) + b1_ref[...]))
            h2 = jnp.sin(sf_ref[1:2, :] * (jnp.dot(h1, w2_ref[...], preferred_element_type=F32) + b2_ref[...]))
            hdn_ref[pl.ds(pl.multiple_of(t2 * n1, n1), n1), :] = h2
            return carry
        lax.fori_loop(0, n2, body, 0)

    @pl.when(kb == 0)
    def _():
        def body(t2hi, asum):
            for t2lo in range(SUBLANE):
                t2 = t2hi * SUBLANE + t2lo
                i, pos = positions(t2)
                h2 = hdn_ref[pl.ds(pl.multiple_of(t2 * n1, n1), n1), :].astype(BF16)
                t_lin = pos / HYENA_TIME_SCALE
                fwd = jnp.dot(h2, w3f_ref[...], preferred_element_type=F32) * jnp.exp(-t_lin * jnp.exp(ldf_ref[...]))
                bwd = jnp.dot(h2, w3b_ref[...], preferred_element_type=F32) * jnp.exp(-t_lin * jnp.exp(ldb_ref[...]))
                k = jnp.where(i < l, fwd, jnp.where(i > l, bwd, 0.0))
                r = jnp.dot(e1_ref[t2], k.astype(BF16), preferred_element_type=F32)
                are_ref[_a_rows(t2hi, t2lo, hp), :] = r[:hp]
                aim_ref[_a_rows(t2hi, t2lo, hp), :] = r[hp:]
                asum = asum + jnp.sum(jnp.abs(k), axis=0, keepdims=True)
            return asum
        asum = lax.fori_loop(0, n2 // SUBLANE, body, jnp.zeros((1, LANE), F32))
        scale_ref[...] = 1.0 / ((asum + FILTER_EPS) * n)

    for kk in range(0, k1b, 2):
        k1 = kb * k1b + kk
        pair = lambda ref: jnp.concatenate([_a_load_k1(ref, k1 + e, n2, hp) for e in range(2)], axis=1)
        rhs = jnp.concatenate([pair(are_ref), pair(aim_ref)], axis=0).astype(BF16)
        x = jnp.dot(f2c_ref[...], rhs, preferred_element_type=F32)
        for e in range(2):
            rows = slice((kk + e) * n2, (kk + e + 1) * n2)
            kr_ref[rows, :] = x[:n2, e * LANE:(e + 1) * LANE] * scale_ref[...]
            ki_ref[rows, :] = x[n2:, e * LANE:(e + 1) * LANE] * scale_ref[...]


def _filter_call(l, p, tb):
    n1, n2, hp = _fft_dims(l)
    k1b = _k1_block(hp)
    n = 2 * l
    nt = HYENA_ORDER * D_HYENA // LANE
    per = D_HYENA // LANE
    fwd_col = lambda j, kb: (0, (j // per) * 2 * per + j % per)
    bwd_col = lambda j, kb: (0, (j // per) * 2 * per + per + j % per)
    spec_out = pl.BlockSpec((k1b * n2, LANE), lambda j, kb: (kb, j))
    a_shape = (n2 * hp, LANE)
    return pl.pallas_call(
        functools.partial(_filter_kernel, l=l, n1=n1, n2=n2, hp=hp, k1b=k1b),
        out_shape=(jax.ShapeDtypeStruct((hp * n2, nt * LANE), F32),) * 2,
        grid=(nt, hp // k1b),
        in_specs=[_resident((LANE, LANE)), _resident((1, LANE)), _resident((LANE, LANE)), _resident((1, LANE)),
                  _resident((2, LANE)), _resident((1, LANE)),
                  pl.BlockSpec((LANE, LANE), fwd_col), pl.BlockSpec((LANE, LANE), bwd_col),
                  pl.BlockSpec((1, LANE), fwd_col), pl.BlockSpec((1, LANE), bwd_col),
                  _resident((n2, 2 * hp, n1)), _resident((2 * n2, 2 * n2))],
        out_specs=(spec_out, spec_out),
        scratch_shapes=[pltpu.VMEM((n, LANE), F32), pltpu.VMEM(a_shape, F32),
                        pltpu.VMEM(a_shape, F32), pltpu.VMEM((1, LANE), F32)],
        compiler_params=_params(("arbitrary", "arbitrary")),
        name="hyena_filter_spectrum",
    )(p["filt_w1"], p["filt_b1"], p["filt_w2"], p["filt_b2"], p["filt_sf"], p["filt_om"],
      p["filt_w3"], p["filt_w3"], p["filt_ld"], p["filt_ld"], tb["e1"], tb["f2c"])


def _short_conv_column(ref, w_ref, b_ref, t2hi, t2lo, n2, half):
    t2 = t2hi * SUBLANE + t2lo
    col = lambda c: ref[:, _t2_rows(c), :].reshape(half, LANE)
    t1 = lax.broadcasted_iota(jnp.int32, (half, LANE), 0)
    before = col(jnp.maximum(t2 - 1, 0))
    if t2lo == 0:
        wrapped = jnp.where(t1 == 0, 0.0, pltpu.roll(col(n2 - 1), 1, 0))
        before = jnp.where(t2hi == 0, wrapped, before)
    after = col(jnp.minimum(t2 + 1, n2 - 1))
    if t2lo == SUBLANE - 1:
        wrapped = jnp.where(t1 == half - 1, 0.0, pltpu.roll(col(0), half - 1, 0))
        after = jnp.where(t2hi == n2 // SUBLANE - 1, wrapped, after)
    return ((b_ref[...] + before * w_ref[0:1, :]) + col(t2) * w_ref[1:2, :]) + after * w_ref[2:3, :]


def _conv_kernel(z_ref, g_ref, bias_ref, zw_ref, zb_ref, gw_ref, gb_ref, e1_ref, einv_ref, f2c_ref, f2ic_ref,
                 kr_ref, ki_ref, o_ref, are_ref, aim_ref, zc_ref, *, n1, n2, hp, k1b, conv_z):
    s = pl.program_id(2)
    half = n1 // 2
    tiles = (half // SUBLANE, SUBLANE, LANE)

    @pl.when(s == 0)
    def _():
        def body(t2hi, carry):
            for t2lo in range(SUBLANE):
                t2 = t2hi * SUBLANE + t2lo
                if conv_z:
                    z = _short_conv_column(z_ref, zw_ref, zb_ref, t2hi, t2lo, n2, half)
                    zc_ref[:, _t2_rows(t2), :] = z.reshape(tiles)
                else:
                    z = z_ref[:, _t2_rows(t2), :].reshape(half, LANE)
                z = z.astype(BF16)
                r = jnp.dot(e1_ref[t2, :, 0:half], z, preferred_element_type=F32)
                are_ref[_a_rows(t2hi, t2lo, hp), :] = r[:hp]
                aim_ref[_a_rows(t2hi, t2lo, hp), :] = r[hp:]
            return carry
        lax.fori_loop(0, n2 // SUBLANE, body, 0, unroll=2)

    for kk in range(0, k1b, 2):
        k1 = s * k1b + kk
        pair = lambda ref: jnp.concatenate([_a_load_k1(ref, k1 + e, n2, hp) for e in range(2)], axis=1)
        rhs = jnp.concatenate([pair(are_ref), pair(aim_ref)], axis=0).astype(BF16)
        x = jnp.dot(f2c_ref[...], rhs, preferred_element_type=F32)
        xr, xi = x[:n2], x[n2:]
        spec = lambda ref: jnp.concatenate([ref[(kk + e) * n2:(kk + e + 1) * n2, :] for e in range(2)], axis=1)
        fr, fi = spec(kr_ref), spec(ki_ref)
        y = jnp.concatenate([xr * fr - xi * fi, xr * fi + xi * fr], axis=0).astype(BF16)
        back = jnp.dot(f2ic_ref[...], y, preferred_element_type=F32)
        for e in range(2):
            _a_store_k1(are_ref, k1 + e, back[:n2, e * LANE:(e + 1) * LANE], n2, hp)
            _a_store_k1(aim_ref, k1 + e, back[n2:, e * LANE:(e + 1) * LANE], n2, hp)

    @pl.when(s == pl.num_programs(2) - 1)
    def _():
        def body(t2hi, carry):
            for t2lo in range(SUBLANE):
                t2 = t2hi * SUBLANE + t2lo
                rows = _a_rows(t2hi, t2lo, hp)
                rhs = jnp.concatenate([are_ref[rows, :], aim_ref[rows, :]], axis=0).astype(BF16)
                y = jnp.dot(einv_ref[t2], rhs, preferred_element_type=F32)
                gate = _short_conv_column(g_ref, gw_ref, gb_ref, t2hi, t2lo, n2, half)
                z = (zc_ref if conv_z else z_ref)[:, _t2_rows(t2), :].reshape(half, LANE)
                o_ref[:, _t2_rows(t2), :] = (gate * (y + bias_ref[...] * z)).reshape(tiles)
            return carry
        lax.fori_loop(0, n2 // SUBLANE, body, 0, unroll=2)


def _conv_call(z, z_col0, g, g_col0, p, kr, ki, order, tb, conv_z):
    b, _, t1_tiles, tt, _ = z.shape
    n2, half = tt // SUBLANE, t1_tiles * SUBLANE
    n1, _, hp = _fft_dims(n2 * half)
    k1b = _k1_block(hp)
    per = D_HYENA // LANE
    seq = lambda col0: pl.BlockSpec((None, None, t1_tiles, tt, LANE), lambda c, i, s: (i, col0 + c, 0, 0, 0))
    taps = lambda col0: pl.BlockSpec((3, LANE), lambda c, i, s: (0, col0 + c))
    shift = lambda col0: pl.BlockSpec((1, LANE), lambda c, i, s: (0, col0 + c))
    spec_k = pl.BlockSpec((k1b * n2, LANE), lambda c, i, s: (s, order * per + c))
    a_shape = (n2 * hp, LANE)
    w_col0 = z_col0 if conv_z else g_col0
    return pl.pallas_call(
        functools.partial(_conv_kernel, n1=n1, n2=n2, hp=hp, k1b=k1b, conv_z=conv_z),
        out_shape=jax.ShapeDtypeStruct((b, per, t1_tiles, tt, LANE), F32),
        grid=(per, b, hp // k1b),
        in_specs=[seq(z_col0), seq(g_col0),
                  pl.BlockSpec((1, LANE), lambda c, i, s: (0, order * per + c)),
                  taps(w_col0), shift(w_col0), taps(g_col0), shift(g_col0),
                  _resident((n2, 2 * hp, n1)), _resident((n2, n1 // 2, 2 * hp)),
                  _resident((2 * n2, 2 * n2)), _resident((2 * n2, 2 * n2)),
                  spec_k, spec_k],
        out_specs=pl.BlockSpec((None, None, t1_tiles, tt, LANE), lambda c, i, s: (i, c, 0, 0, 0)),
        scratch_shapes=[pltpu.VMEM(a_shape, F32), pltpu.VMEM(a_shape, F32),
                        pltpu.VMEM((t1_tiles, tt, LANE) if conv_z else (SUBLANE, LANE), F32)],
        compiler_params=_params(("parallel", "parallel", "arbitrary")),
        name=f"hyena_conv_order{order}",
    )(z, g, p["hy_bias"], p["short_w"], p["short_b"], p["short_w"], p["short_b"],
      tb["e1"], tb["einv"], tb["f2c"], tb["f2ic"], kr, ki)


def _prepare(ffn1_w_gate, ffn1_w_up, ffn1_w_down, ln1_g, ln1_b, w_in,
             ssm_lam_re, ssm_lam_im, ssm_log_step, ssm_b_re, ssm_b_im, ssm_c_re, ssm_c_im,
             ssm_d, ssm_glu_w, ssm_glu_b, ssm_norm_g,
             hy_short_w, hy_short_b, hy_filt_w1, hy_filt_b1, hy_filt_w2, hy_filt_b2, hy_filt_w3,
             hy_sin_freq, hy_log_decay, hy_bias, hy_norm_g, w_out, ln2_g, ln2_b,
             ffn2_w_gate, ffn2_w_up, ffn2_w_down, ln3_g, ln3_b):
    row = lambda a: a[0].reshape(1, -1).astype(F32)
    hid = HYENA_FILTER_HIDDEN
    pad2 = lambda a, r, c: jnp.zeros((r, c), F32).at[:a.shape[0], :a.shape[1]].set(a.astype(F32))
    omega = jnp.exp(-math.log(HYENA_MAX_PERIOD) * jnp.arange(HYENA_BANDS, dtype=F32) / HYENA_BANDS)
    om = jnp.zeros((1, LANE), F32).at[0, 1:1 + HYENA_BANDS].set(omega).at[0, 1 + HYENA_BANDS:HYENA_POS_DIM].set(omega)
    p = dict(
        ffn1_wg=ffn1_w_gate[0].astype(BF16), ffn1_wu=ffn1_w_up[0].astype(BF16), ffn1_wd=ffn1_w_down[0].astype(BF16),
        ln1_g=row(ln1_g), ln1_b=row(ln1_b), w_in=w_in[0].astype(BF16),
        glu_w=ssm_glu_w[0].astype(BF16), glu_b=row(ssm_glu_b), ssm_norm_g=row(ssm_norm_g),
        hy_norm_g=row(hy_norm_g), w_out=w_out[0].astype(BF16), ln2_g=row(ln2_g), ln2_b=row(ln2_b),
        ffn2_wg=ffn2_w_gate[0].astype(BF16), ffn2_wu=ffn2_w_up[0].astype(BF16), ffn2_wd=ffn2_w_down[0].astype(BF16),
        ln3_g=row(ln3_g), ln3_b=row(ln3_b),
        short_w=hy_short_w[0].astype(F32), short_b=row(hy_short_b),
        filt_w1=pad2(hy_filt_w1[0], LANE, LANE), filt_b1=pad2(hy_filt_b1[0].reshape(1, hid), 1, LANE),
        filt_w2=pad2(hy_filt_w2[0], LANE, LANE), filt_b2=pad2(hy_filt_b2[0].reshape(1, hid), 1, LANE),
        filt_sf=pad2(hy_sin_freq[0], 2, LANE), filt_om=om,
        filt_w3=pad2(hy_filt_w3[0], LANE, hy_filt_w3.shape[-1]).astype(BF16), filt_ld=row(hy_log_decay),
        hy_bias=hy_bias[0].reshape(1, HYENA_ORDER * D_HYENA).astype(F32),
    )
    f32 = lambda a: a[0].astype(F32)
    p.update(_s5_tables(f32(ssm_lam_re), f32(ssm_lam_im), f32(ssm_log_step), f32(ssm_b_re), f32(ssm_b_im),
                        f32(ssm_c_re), f32(ssm_c_im), f32(ssm_d)))
    return p


def _trunk(x, p):
    b, l, d = x.shape
    per = D_HYENA // LANE
    h1, u, hy_raw = _ffn1_call(x, p)
    y_ssm = _s5_call(u.reshape(b, l, D_SSM), p)
    tb = _dft_tables(l)
    kr, ki = _filter_call(l, p, tb)
    z1 = _conv_call(hy_raw, 0, hy_raw, per, p, kr, ki, 0, tb, True)
    z2 = _conv_call(z1, 0, hy_raw, 2 * per, p, kr, ki, 1, tb, False)
    return _final_call(y_ssm, z2, h1.reshape(b, l, d), p)


def kernel(x_prompt, x_sample, ffn1_w_gate, ffn1_w_up, ffn1_w_down, ln1_g, ln1_b, w_in, ssm_lam_re, ssm_lam_im, ssm_log_step, ssm_b_re, ssm_b_im, ssm_c_re, ssm_c_im, ssm_d, ssm_glu_w, ssm_glu_b, ssm_norm_g, hy_short_w, hy_short_b, hy_filt_w1, hy_filt_b1, hy_filt_w2, hy_filt_b2, hy_filt_w3, hy_sin_freq, hy_log_decay, hy_bias, hy_norm_g, w_out, ln2_g, ln2_b, ffn2_w_gate, ffn2_w_up, ffn2_w_down, ln3_g, ln3_b):
    p = _prepare(ffn1_w_gate, ffn1_w_up, ffn1_w_down, ln1_g, ln1_b, w_in,
                 ssm_lam_re, ssm_lam_im, ssm_log_step, ssm_b_re, ssm_b_im, ssm_c_re, ssm_c_im,
                 ssm_d, ssm_glu_w, ssm_glu_b, ssm_norm_g,
                 hy_short_w, hy_short_b, hy_filt_w1, hy_filt_b1, hy_filt_w2, hy_filt_b2, hy_filt_w3,
                 hy_sin_freq, hy_log_decay, hy_bias, hy_norm_g, w_out, ln2_g, ln2_b,
                 ffn2_w_gate, ffn2_w_up, ffn2_w_down, ln3_g, ln3_b)
    return (_trunk(x_prompt, p), _trunk(x_sample, p))
```

```python
import functools
import math

import jax
import jax.numpy as jnp
from jax import lax
from jax.experimental import pallas as pl
from jax.experimental.pallas import tpu as pltpu

F32 = jnp.float32
BF16 = jnp.bfloat16

D_MODEL = 1024
D_SSM = 512
SSM_GROUP = 16
N_SSM_GROUPS = D_SSM // SSM_GROUP
SSM_STATE = 64
D_HYENA = 512
HYENA_ORDER = 2
HYENA_BANDS = 8
HYENA_POS_DIM = 1 + 2 * HYENA_BANDS
HYENA_FILTER_HIDDEN = 64
HYENA_TIME_SCALE = 4096.0
HYENA_MAX_PERIOD = 10000.0
D_HY_IN = (HYENA_ORDER + 1) * D_HYENA
D_IN = D_SSM + D_HY_IN
D_FF = 128 * math.ceil(8 * D_MODEL / 3 / 128)
LN_EPS = 1e-5
RMS_EPS = 1e-6
FILTER_EPS = 1e-6
DEPTH = 1
DEEPNORM_ALPHA = (2.0 * DEPTH) ** 0.25

LANE = 128
SUBLANE = 8
V7X_VMEM_BYTES = 64 * 1024 * 1024
VMEM_LIMIT_BYTES = V7X_VMEM_BYTES - 8 * 1024 * 1024

TOKEN_TILE = 512
FF_CHUNK = 256
S5_CHUNK = 16
S5_LANES = S5_CHUNK * LANE
S5_STATE_LANES = (LANE // SSM_GROUP) * SSM_STATE
K1_BLOCK_MAX = 24


def _params(semantics):
    return pltpu.CompilerParams(dimension_semantics=semantics, vmem_limit_bytes=VMEM_LIMIT_BYTES)


def _resident(shape):
    nd = len(shape)
    return pl.BlockSpec(shape, lambda *_: (0,) * nd, pipeline_mode=pl.Buffered(1))


def _layer_norm(x, g, b):
    mu = jnp.mean(x, axis=-1, keepdims=True)
    xc = x - mu
    var = jnp.mean(xc * xc, axis=-1, keepdims=True)
    return xc * lax.rsqrt(var + LN_EPS) * g + b


def _rms_norm(x, g):
    ms = jnp.mean(x * x, axis=-1, keepdims=True)
    return x * lax.rsqrt(ms + RMS_EPS) * g


def _ffn_ln(x, wg_ref, wu_ref, wd_ref, g_ref, b_ref, acc_ref):
    xb = x.astype(BF16)
    acc_ref[...] = jnp.zeros_like(acc_ref)

    def body(c, carry):
        off = pl.multiple_of(c * FF_CHUNK, FF_CHUNK)
        gate = jnp.dot(xb, wg_ref[:, pl.ds(off, FF_CHUNK)], preferred_element_type=F32)
        up = jnp.dot(xb, wu_ref[:, pl.ds(off, FF_CHUNK)], preferred_element_type=F32)
        act = (jax.nn.silu(gate) * up).astype(BF16)
        acc_ref[...] += jnp.dot(act, wd_ref[pl.ds(off, FF_CHUNK), :], preferred_element_type=F32)
        return carry

    lax.fori_loop(0, D_FF // FF_CHUNK, body, 0, unroll=True)
    return _layer_norm(DEEPNORM_ALPHA * x + 0.5 * acc_ref[...], g_ref[...], b_ref[...])


def _ffn1_kernel(x_ref, wg_ref, wu_ref, wd_ref, g_ref, b_ref, win_ref, h_ref, u_ref, hy_ref, acc_ref, *, n2):
    h = _ffn_ln(x_ref[...], wg_ref, wu_ref, wd_ref, g_ref, b_ref, acc_ref)
    h_ref[...] = h
    hb = h.astype(BF16)
    u_ref[...] = jnp.dot(hb, win_ref[:, :D_SSM], preferred_element_type=F32)
    hy = jnp.dot(hb, win_ref[:, D_SSM:], preferred_element_type=F32)
    per = TOKEN_TILE // n2
    first = (pl.program_id(0) % (SUBLANE // per)) * per
    for ct in range(D_HY_IN // LANE):
        for t1 in range(per):
            hy_ref[ct, pl.ds(first + t1, n2, stride=SUBLANE), :] = hy[t1 * n2:(t1 + 1) * n2, ct * LANE:(ct + 1) * LANE]


def _ffn1_call(x, p):
    b, l, _ = x.shape
    _, n2, _ = _fft_dims(l)
    tm = TOKEN_TILE
    assert l % tm == 0 and tm % n2 == 0 and SUBLANE % (tm // n2) == 0
    t = b * l
    steps = SUBLANE * n2 // tm
    tiles = l // tm
    row = lambda w: pl.BlockSpec((tm, w), lambda i: (i, 0))
    return pl.pallas_call(
        functools.partial(_ffn1_kernel, n2=n2),
        out_shape=(jax.ShapeDtypeStruct((t, D_MODEL), F32),
                   jax.ShapeDtypeStruct((t, D_SSM), F32),
                   jax.ShapeDtypeStruct((b, D_HY_IN // LANE, l // (SUBLANE * n2), SUBLANE * n2, LANE), F32)),
        grid=(t // tm,),
        in_specs=[row(D_MODEL), _resident((D_MODEL, D_FF)), _resident((D_MODEL, D_FF)),
                  _resident((D_FF, D_MODEL)), _resident((1, D_MODEL)), _resident((1, D_MODEL)),
                  _resident((D_MODEL, D_IN))],
        out_specs=(row(D_MODEL), row(D_SSM),
                   pl.BlockSpec((None, D_HY_IN // LANE, None, SUBLANE * n2, LANE),
                                lambda i: (i // tiles, 0, (i % tiles) // steps, 0, 0))),
        scratch_shapes=[pltpu.VMEM((tm, D_MODEL), F32)],
        compiler_params=_params(("arbitrary",)),
        name="ffn1_ln1_proj",
    )(x.reshape(t, D_MODEL), p["ffn1_wg"], p["ffn1_wu"], p["ffn1_wd"], p["ln1_g"], p["ln1_b"], p["w_in"])


def _final_kernel(ys_ref, yh_ref, h_ref, gw_ref, gb_ref, sg_ref, hg_ref, wo_ref, l2g_ref, l2b_ref,
                  wg_ref, wu_ref, wd_ref, l3g_ref, l3b_ref, o_ref, acc_ref, *, n2):
    per = TOKEN_TILE // n2

    def sub_tile(hh, carry):
        rows = pl.ds(pl.multiple_of(hh * TOKEN_TILE, TOKEN_TILE), TOKEN_TILE)
        y_hy = jnp.concatenate(
            [jnp.concatenate([yh_ref[ct, pl.ds(hh * per + jj, n2, stride=SUBLANE), :]
                              for ct in range(D_HYENA // LANE)], axis=1) for jj in range(per)], axis=0)
        g = jax.nn.gelu(ys_ref[rows, :])
        gate = jax.nn.sigmoid(jnp.dot(g.astype(BF16), gw_ref[...], preferred_element_type=F32) + gb_ref[...])
        y_ssm = _rms_norm(g * gate, sg_ref[...])
        y_hy = _rms_norm(y_hy, hg_ref[...])
        mix = (jnp.dot(y_ssm.astype(BF16), wo_ref[:D_SSM, :], preferred_element_type=F32)
               + jnp.dot(y_hy.astype(BF16), wo_ref[D_SSM:, :], preferred_element_type=F32))
        x2 = _layer_norm(DEEPNORM_ALPHA * h_ref[rows, :] + mix, l2g_ref[...], l2b_ref[...])
        o_ref[rows, :] = _ffn_ln(x2, wg_ref, wu_ref, wd_ref, l3g_ref, l3b_ref, acc_ref)
        return carry

    lax.fori_loop(0, SUBLANE * n2 // TOKEN_TILE, sub_tile, 0)


def _final_call(y_ssm, y_hy, h1, p):
    b, l, _ = y_ssm.shape
    tt = y_hy.shape[3]
    n2 = tt // SUBLANE
    assert tt % TOKEN_TILE == 0 and TOKEN_TILE % n2 == 0
    row = lambda w: pl.BlockSpec((None, tt, w), lambda i, a: (i, a, 0))
    return pl.pallas_call(
        functools.partial(_final_kernel, n2=n2),
        out_shape=jax.ShapeDtypeStruct((b, l, D_MODEL), F32),
        grid=(b, l // tt),
        in_specs=[row(D_SSM),
                  pl.BlockSpec((None, D_HYENA // LANE, None, tt, LANE), lambda i, a: (i, 0, a, 0, 0)),
                  row(D_MODEL),
                  _resident((D_SSM, D_SSM)), _resident((1, D_SSM)), _resident((1, D_SSM)),
                  _resident((1, D_HYENA)), _resident((D_MODEL, D_MODEL)),
                  _resident((1, D_MODEL)), _resident((1, D_MODEL)),
                  _resident((D_MODEL, D_FF)), _resident((D_MODEL, D_FF)), _resident((D_FF, D_MODEL)),
                  _resident((1, D_MODEL)), _resident((1, D_MODEL))],
        out_specs=row(D_MODEL),
        scratch_shapes=[pltpu.VMEM((TOKEN_TILE, D_MODEL), F32)],
        compiler_params=_params(("parallel", "parallel")),
        name="mix_ln2_ffn2_ln3",
    )(y_ssm, y_hy, h1, p["glu_w"], p["glu_b"], p["ssm_norm_g"], p["hy_norm_g"], p["w_out"],
      p["ln2_g"], p["ln2_b"], p["ffn2_wg"], p["ffn2_wu"], p["ffn2_wd"], p["ln3_g"], p["ln3_b"])


def _s5_kernel(u_ref, blk_ref, ein_ref, gout_ref, lam_ref, d_ref, y_ref, ucat_ref, st_ref, m_ref, *, nch, rb):
    sl = S5_STATE_LANES

    @pl.when(pl.program_id(1) == 0)
    def _():
        for t_in in range(S5_CHUNK):
            for t_out in range(S5_CHUNK):
                m_ref[t_in * LANE:(t_in + 1) * LANE, t_out * LANE:(t_out + 1) * LANE] = (
                    blk_ref[t_out - t_in + S5_CHUNK - 1])

    for t in range(S5_CHUNK):
        ucat_ref[:, t * LANE:(t + 1) * LANE] = u_ref[pl.ds(t, nch, stride=S5_CHUNK), :].astype(BF16)

    def state_in(i, carry):
        r0 = pl.multiple_of(i * rb, rb)
        lhs = ucat_ref[pl.ds(r0, rb), :]
        for cb in range(4):
            st_ref[pl.ds(r0, rb), cb * sl:(cb + 1) * sl] = jnp.dot(
                lhs, ein_ref[:, cb * sl:(cb + 1) * sl], preferred_element_type=F32)
        return carry

    lax.fori_loop(0, nch // rb, state_in, 0)

    lfr, lfi = lam_ref[0:1, :], lam_ref[1:2, :]
    lbr, lbi = lam_ref[2:3, :], lam_ref[3:4, :]

    def scan(i, carry):
        sfr, sfi, sbr, sbi = carry
        j = nch - 1 - i
        xfr = st_ref[pl.ds(i, 1), 0:sl]
        xfi = st_ref[pl.ds(i, 1), sl:2 * sl]
        st_ref[pl.ds(i, 1), 0:sl] = sfr
        st_ref[pl.ds(i, 1), sl:2 * sl] = sfi
        xbr = st_ref[pl.ds(j, 1), 2 * sl:3 * sl]
        xbi = st_ref[pl.ds(j, 1), 3 * sl:4 * sl]
        st_ref[pl.ds(j, 1), 2 * sl:3 * sl] = sbr
        st_ref[pl.ds(j, 1), 3 * sl:4 * sl] = sbi
        return (lfr * sfr - lfi * sfi + xfr, lfr * sfi + lfi * sfr + xfi,
                lbr * sbr - lbi * sbi + xbr, lbr * sbi + lbi * sbr + xbi)

    zero = jnp.zeros((1, sl), F32)
    lax.fori_loop(0, nch, scan, (zero, zero, zero, zero))

    def emit(i, carry):
        r0 = pl.multiple_of(i * rb, rb)
        lhs_u = ucat_ref[pl.ds(r0, rb), :]
        lhs_s = st_ref[pl.ds(r0, rb), :].astype(BF16)
        for cb in range(4):
            cols = slice(cb * 4 * LANE, (cb + 1) * 4 * LANE)
            yc = (jnp.dot(lhs_u, m_ref[:, cols], preferred_element_type=F32)
                  + jnp.dot(lhs_s, gout_ref[:, cols], preferred_element_type=F32))
            for tt in range(4):
                rows = pl.ds(r0 * S5_CHUNK + cb * 4 + tt, rb, stride=S5_CHUNK)
                y_ref[rows, :] = yc[:, tt * LANE:(tt + 1) * LANE] + d_ref[...] * u_ref[rows, :]
        return carry

    lax.fori_loop(0, nch // rb, emit, 0)


def _s5_call(u, p):
    b, l, _ = u.shape
    nch = l // S5_CHUNK
    rb = min(nch, 512)
    nq = D_SSM // LANE
    mat = pl.BlockSpec((None, S5_LANES, S5_LANES), lambda q, i: (q, 0, 0), pipeline_mode=pl.Buffered(1))
    lags = 2 * S5_CHUNK - 1
    return pl.pallas_call(
        functools.partial(_s5_kernel, nch=nch, rb=rb),
        out_shape=jax.ShapeDtypeStruct((b, l, D_SSM), F32),
        grid=(nq, b),
        in_specs=[pl.BlockSpec((None, l, LANE), lambda q, i: (i, 0, q)),
                  pl.BlockSpec((None, lags, LANE, LANE), lambda q, i: (q, 0, 0, 0), pipeline_mode=pl.Buffered(1)),
                  mat, mat,
                  pl.BlockSpec((None, 4, S5_STATE_LANES), lambda q, i: (q, 0, 0)),
                  pl.BlockSpec((None, 1, LANE), lambda q, i: (q, 0, 0))],
        out_specs=pl.BlockSpec((None, l, LANE), lambda q, i: (i, 0, q)),
        scratch_shapes=[pltpu.VMEM((nch, S5_LANES), BF16), pltpu.VMEM((nch, 4 * S5_STATE_LANES), F32),
                        pltpu.VMEM((S5_LANES, S5_LANES), BF16)],
        compiler_params=_params(("arbitrary", "arbitrary")),
        name="s5_chunked",
    )(u, p["s5_blk"], p["s5_ein"], p["s5_gout"], p["s5_lam"], p["s5_d"])


def _cmul(ar, ai, br, bi):
    return ar * br - ai * bi, ar * bi + ai * br


def _s5_tables(lam_re, lam_im, log_step, b_re, b_im, c_re, c_im, d):
    g, hh, pp, tc = N_SSM_GROUPS, SSM_GROUP, SSM_STATE, S5_CHUNK
    gl = LANE // hh
    nq = g // gl
    step = jnp.exp(log_step)[..., None]
    mag = jnp.exp(lam_re * step)
    ar = mag * jnp.cos(lam_im * step)
    ai = mag * jnp.sin(lam_im * step)
    nr, ni = ar - 1.0, ai
    den = lam_re * lam_re + lam_im * lam_im
    qr = (nr * lam_re + ni * lam_im) / den
    qi = (ni * lam_re - nr * lam_im) / den
    bbr = qr[..., None] * b_re - qi[..., None] * b_im
    bbi = qr[..., None] * b_im + qi[..., None] * b_re
    pr, pi = [jnp.ones_like(ar)], [jnp.zeros_like(ai)]
    for _ in range(tc):
        r, i = _cmul(pr[-1], pi[-1], ar, ai)
        pr.append(r)
        pi.append(i)
    pwr, pwi = jnp.stack(pr), jnp.stack(pi)

    def c_times_pow(dd, sel):
        return _cmul(c_re[dd][None], c_im[dd][None], pwr[sel, dd][:, :, None, :], pwi[sel, dd][:, :, None, :])

    def lag_kernels(dd):
        zr, zi = (jnp.moveaxis(z, 3, 0)[..., None] for z in c_times_pow(dd, jnp.arange(tc)))
        br, bi = (jnp.moveaxis(bb[dd], 1, 0)[:, None, :, None, :] for bb in (bbr, bbi))
        return jnp.sum(zr * br - zi * bi, axis=0)

    kf, kb = lag_kernels(0), lag_kernels(1)
    kall = jnp.concatenate([kb[jnp.arange(tc - 1, 0, -1)], kf[:1] + kb[:1], kf[1:]], axis=0)

    def expand(compact, src_col, row_grp, col_grp):
        rep = (jnp.arange(compact.shape[1])[:, None] == src_col[None, :]).astype(BF16)
        full = jnp.dot(compact.astype(BF16), rep, preferred_element_type=BF16)
        return jnp.where(row_grp[:, None] == col_grp[None, :], full, jnp.zeros((), BF16))

    lane = jnp.arange(LANE)
    chunk_lane = jnp.arange(S5_LANES)

    k2 = kall.transpose(0, 1, 3, 2).reshape((2 * tc - 1) * g * hh, hh)
    blk = expand(k2, lane % hh, (jnp.arange(k2.shape[0]) % LANE) // hh, lane // hh)
    blk = blk.reshape(2 * tc - 1, nq, LANE, LANE).transpose(1, 0, 2, 3)

    bbr_t, bbi_t = bbr.transpose(0, 1, 3, 2), bbi.transpose(0, 1, 3, 2)

    def state_in(dd, sel):
        return _cmul(pwr[sel, dd][:, :, None, :], pwi[sel, dd][:, :, None, :], bbr_t[dd][None], bbi_t[dd][None])

    cin = jnp.concatenate([c.reshape(tc, nq, LANE, pp)
                           for c in state_in(0, jnp.arange(tc - 1, -1, -1))
                           + state_in(1, jnp.arange(tc))], axis=3)
    cin = cin.transpose(1, 0, 2, 3).reshape(nq * S5_LANES, 4 * pp)
    state_col = jnp.arange(4 * gl * pp)
    ein = expand(cin, (state_col // (gl * pp)) * pp + state_col % pp,
                 (jnp.arange(nq * S5_LANES) % LANE) // hh, (state_col % (gl * pp)) // pp)
    ein = ein.reshape(nq, S5_LANES, 4 * gl * pp)

    ofr, ofi = c_times_pow(0, jnp.arange(1, tc + 1))
    obr, obi = c_times_pow(1, jnp.arange(tc, 0, -1))
    cout = jnp.stack([c.transpose(1, 3, 0, 2).reshape(nq, gl * pp, tc * hh) for c in (ofr, -ofi, obr, -obi)], axis=1)
    cout = cout.reshape(nq * 4 * gl * pp, tc * hh)
    gout = expand(cout, (chunk_lane // LANE) * hh + chunk_lane % hh,
                  (jnp.arange(cout.shape[0]) % (gl * pp)) // pp, (chunk_lane % LANE) // hh)
    gout = gout.reshape(nq, 4 * gl * pp, S5_LANES)

    lam16 = jnp.stack([pwr[tc, 0], pwi[tc, 0], pwr[tc, 1], pwi[tc, 1]], axis=0)
    lam16 = lam16.reshape(4, nq, gl * pp).transpose(1, 0, 2)
    return dict(s5_blk=blk, s5_ein=ein, s5_gout=gout,
                s5_lam=lam16.astype(F32), s5_d=d.reshape(nq, 1, LANE).astype(F32))


def _fft_dims(l):
    n = 2 * l
    bits = n.bit_length() - 1
    assert n == 1 << bits and bits % 2 == 0, "sequence length must give a square power-of-two DFT size"
    n1 = 1 << (bits // 2)
    hp = -(-(n1 // 2 + 1) // SUBLANE) * SUBLANE
    return n1, n1, hp


def _t2_rows(t2):
    return pl.ds(pl.multiple_of(t2 * SUBLANE, SUBLANE), SUBLANE)


def _a_rows(t2hi, t2lo, hp):
    return pl.ds(t2hi * (hp * SUBLANE) + t2lo, hp, stride=SUBLANE)


def _a_tile(t2hi, k1, hp):
    return pl.ds(pl.multiple_of((t2hi * hp + k1) * SUBLANE, SUBLANE), SUBLANE)


def _a_load_k1(ref, k1, n2, hp):
    return jnp.concatenate([ref[_a_tile(t, k1, hp), :] for t in range(n2 // SUBLANE)], axis=0)


def _a_store_k1(ref, k1, val, n2, hp):
    for t in range(n2 // SUBLANE):
        ref[_a_tile(t, k1, hp), :] = val[t * SUBLANE:(t + 1) * SUBLANE, :]


def _k1_block(hp):
    return max(d for d in range(2, K1_BLOCK_MAX + 1, 2) if hp % d == 0)


def _dft_tables(l):
    n = 2 * l
    n1, n2, hp = _fft_dims(l)
    t2 = jnp.arange(n2, dtype=jnp.int32)[:, None, None]
    k1 = jnp.arange(hp, dtype=jnp.int32)[None, :, None]
    t1 = jnp.arange(n1, dtype=jnp.int32)[None, None, :]
    ang_a = (2.0 * math.pi / n1) * ((t1 * k1) % n1).astype(F32)
    ang_b = (2.0 * math.pi / n) * ((t2 * k1) % n).astype(F32)
    cos1 = jnp.cos(ang_a) * jnp.cos(ang_b) - jnp.sin(ang_a) * jnp.sin(ang_b)
    sin1 = jnp.sin(ang_a) * jnp.cos(ang_b) + jnp.cos(ang_a) * jnp.sin(ang_b)
    e1 = jnp.concatenate([cos1, -sin1], axis=1)
    wgt = jnp.where((k1 == 0) | (k1 == n1 // 2), 1.0, jnp.where(k1 < n1 // 2, 2.0, 0.0))
    half = n1 // 2
    einv = jnp.concatenate([(wgt * cos1)[:, :, :half], (-wgt * sin1)[:, :, :half]], axis=1)
    einv = einv.transpose(0, 2, 1)
    a = jnp.arange(n2, dtype=jnp.int32)
    ang2 = (2.0 * math.pi / n2) * ((a[:, None] * a[None, :]) % n2).astype(F32)
    c2, s2 = jnp.cos(ang2), jnp.sin(ang2)
    f2c = jnp.block([[c2, s2], [-s2, c2]])
    f2ic = jnp.block([[c2, -s2], [s2, c2]])
    return dict(e1=e1.astype(BF16), einv=einv.astype(BF16), f2c=f2c.astype(BF16), f2ic=f2ic.astype(BF16))


def _filter_kernel(w1_ref, b1_ref, w2_ref, b2_ref, sf_ref, om_ref, w3f_ref, w3b_ref, ldf_ref, ldb_ref,
                   e1_ref, f2c_ref, kr_ref, ki_ref, hdn_ref, are_ref, aim_ref, scale_ref, *, l, n1, n2, hp, k1b):
    j = pl.program_id(0)
    kb = pl.program_id(1)
    n = 2 * l

    def positions(t2):
        t1 = lax.broadcasted_iota(jnp.int32, (n1, LANE), 0)
        i = t1 * n2 + t2
        return i, jnp.where(i <= l, i, n - i).astype(F32)

    @pl.when((j == 0) & (kb == 0))
    def _():
        def body(t2, carry):
            _, pos = positions(t2)
            ang = pos * om_ref[...]
            lane = lax.broadcasted_iota(jnp.int32, (n1, LANE), 1)
            feats = jnp.where(lane == 0, pos / HYENA_TIME_SCALE,
                              jnp.where(lane <= HYENA_BANDS, jnp.sin(ang),
                                        jnp.where(lane <= 2 * HYENA_BANDS, jnp.cos(ang), 0.0)))
            h1 = jnp.sin(sf_ref[0:1, :] * (jnp.dot(feats, w1_ref[...], preferred_element_type=F32) + b1_ref[...]))
            h2 = jnp.sin(sf_ref[1:2, :] * (jnp.dot(h1, w2_ref[...], preferred_element_type=F32) + b2_ref[...]))
            hdn_ref[pl.ds(pl.multiple_of(t2 * n1, n1), n1), :] = h2
            return carry
        lax.fori_loop(0, n2, body, 0)

    @pl.when(kb == 0)
    def _():
        def body(t2hi, asum):
            for t2lo in range(SUBLANE):
                t2 = t2hi * SUBLANE + t2lo
                i, pos = positions(t2)
                h2 = hdn_ref[pl.ds(pl.multiple_of(t2 * n1, n1), n1), :].astype(BF16)
                t_lin = pos / HYENA_TIME_SCALE
                fwd = jnp.dot(h2, w3f_ref[...], preferred_element_type=F32) * jnp.exp(-t_lin * jnp.exp(ldf_ref[...]))
                bwd = jnp.dot(h2, w3b_ref[...], preferred_element_type=F32) * jnp.exp(-t_lin * jnp.exp(ldb_ref[...]))
                k = jnp.where(i < l, fwd, jnp.where(i > l, bwd, 0.0))
                r = jnp.dot(e1_ref[t2], k.astype(BF16), preferred_element_type=F32)
                are_ref[_a_rows(t2hi, t2lo, hp), :] = r[:hp]
                aim_ref[_a_rows(t2hi, t2lo, hp), :] = r[hp:]
                asum = asum + jnp.sum(jnp.abs(k), axis=0, keepdims=True)
            return asum
        asum = lax.fori_loop(0, n2 // SUBLANE, body, jnp.zeros((1, LANE), F32))
        scale_ref[...] = 1.0 / ((asum + FILTER_EPS) * n)

    for kk in range(0, k1b, 2):
        k1 = kb * k1b + kk
        pair = lambda ref: jnp.concatenate([_a_load_k1(ref, k1 + e, n2, hp) for e in range(2)], axis=1)
        rhs = jnp.concatenate([pair(are_ref), pair(aim_ref)], axis=0).astype(BF16)
        x = jnp.dot(f2c_ref[...], rhs, preferred_element_type=F32)
        for e in range(2):
            rows = slice((kk + e) * n2, (kk + e + 1) * n2)
            kr_ref[rows, :] = x[:n2, e * LANE:(e + 1) * LANE] * scale_ref[...]
            ki_ref[rows, :] = x[n2:, e * LANE:(e + 1) * LANE] * scale_ref[...]


def _filter_call(l, p, tb):
    n1, n2, hp = _fft_dims(l)
    k1b = _k1_block(hp)
    n = 2 * l
    nt = HYENA_ORDER * D_HYENA // LANE
    per = D_HYENA // LANE
    fwd_col = lambda j, kb: (0, (j // per) * 2 * per + j % per)
    bwd_col = lambda j, kb: (0, (j // per) * 2 * per + per + j % per)
    spec_out = pl.BlockSpec((k1b * n2, LANE), lambda j, kb: (kb, j))
    a_shape = (n2 * hp, LANE)
    return pl.pallas_call(
        functools.partial(_filter_kernel, l=l, n1=n1, n2=n2, hp=hp, k1b=k1b),
        out_shape=(jax.ShapeDtypeStruct((hp * n2, nt * LANE), F32),) * 2,
        grid=(nt, hp // k1b),
        in_specs=[_resident((LANE, LANE)), _resident((1, LANE)), _resident((LANE, LANE)), _resident((1, LANE)),
                  _resident((2, LANE)), _resident((1, LANE)),
                  pl.BlockSpec((LANE, LANE), fwd_col), pl.BlockSpec((LANE, LANE), bwd_col),
                  pl.BlockSpec((1, LANE), fwd_col), pl.BlockSpec((1, LANE), bwd_col),
                  _resident((n2, 2 * hp, n1)), _resident((2 * n2, 2 * n2))],
        out_specs=(spec_out, spec_out),
        scratch_shapes=[pltpu.VMEM((n, LANE), F32), pltpu.VMEM(a_shape, F32),
                        pltpu.VMEM(a_shape, F32), pltpu.VMEM((1, LANE), F32)],
        compiler_params=_params(("arbitrary", "arbitrary")),
        name="hyena_filter_spectrum",
    )(p["filt_w1"], p["filt_b1"], p["filt_w2"], p["filt_b2"], p["filt_sf"], p["filt_om"],
      p["filt_w3"], p["filt_w3"], p["filt_ld"], p["filt_ld"], tb["e1"], tb["f2c"])


def _short_conv_column(ref, w_ref, b_ref, t2hi, t2lo, n2, half):
    t2 = t2hi * SUBLANE + t2lo
    col = lambda c: ref[:, _t2_rows(c), :].reshape(half, LANE)
    t1 = lax.broadcasted_iota(jnp.int32, (half, LANE), 0)
    before = col(jnp.maximum(t2 - 1, 0))
    if t2lo == 0:
        wrapped = jnp.where(t1 == 0, 0.0, pltpu.roll(col(n2 - 1), 1, 0))
        before = jnp.where(t2hi == 0, wrapped, before)
    after = col(jnp.minimum(t2 + 1, n2 - 1))
    if t2lo == SUBLANE - 1:
        wrapped = jnp.where(t1 == half - 1, 0.0, pltpu.roll(col(0), half - 1, 0))
        after = jnp.where(t2hi == n2 // SUBLANE - 1, wrapped, after)
    return ((b_ref[...] + before * w_ref[0:1, :]) + col(t2) * w_ref[1:2, :]) + after * w_ref[2:3, :]


def _conv_kernel(z_ref, g_ref, bias_ref, zw_ref, zb_ref, gw_ref, gb_ref, e1_ref, einv_ref, f2c_ref, f2ic_ref,
                 kr_ref, ki_ref, o_ref, are_ref, aim_ref, zc_ref, *, n1, n2, hp, k1b, conv_z):
    s = pl.program_id(2)
    half = n1 // 2
    tiles = (half // SUBLANE, SUBLANE, LANE)

    @pl.when(s == 0)
    def _():
        def body(t2hi, carry):
            for t2lo in range(SUBLANE):
                t2 = t2hi * SUBLANE + t2lo
                if conv_z:
                    z = _short_conv_column(z_ref, zw_ref, zb_ref, t2hi, t2lo, n2, half)
                    zc_ref[:, _t2_rows(t2), :] = z.reshape(tiles)
                else:
                    z = z_ref[:, _t2_rows(t2), :].reshape(half, LANE)
                z = z.astype(BF16)
                r = jnp.dot(e1_ref[t2, :, 0:half], z, preferred_element_type=F32)
                are_ref[_a_rows(t2hi, t2lo, hp), :] = r[:hp]
                aim_ref[_a_rows(t2hi, t2lo, hp), :] = r[hp:]
            return carry
        lax.fori_loop(0, n2 // SUBLANE, body, 0, unroll=2)

    for kk in range(0, k1b, 2):
        k1 = s * k1b + kk
        pair = lambda ref: jnp.concatenate([_a_load_k1(ref, k1 + e, n2, hp) for e in range(2)], axis=1)
        rhs = jnp.concatenate([pair(are_ref), pair(aim_ref)], axis=0).astype(BF16)
        x = jnp.dot(f2c_ref[...], rhs, preferred_element_type=F32)
        xr, xi = x[:n2], x[n2:]
        spec = lambda ref: jnp.concatenate([ref[(kk + e) * n2:(kk + e + 1) * n2, :] for e in range(2)], axis=1)
        fr, fi = spec(kr_ref), spec(ki_ref)
        y = jnp.concatenate([xr * fr - xi * fi, xr * fi + xi * fr], axis=0).astype(BF16)
        back = jnp.dot(f2ic_ref[...], y, preferred_element_type=F32)
        for e in range(2):
            _a_store_k1(are_ref, k1 + e, back[:n2, e * LANE:(e + 1) * LANE], n2, hp)
            _a_store_k1(aim_ref, k1 + e, back[n2:, e * LANE:(e + 1) * LANE], n2, hp)

    @pl.when(s == pl.num_programs(2) - 1)
    def _():
        def body(t2hi, carry):
            for t2lo in range(SUBLANE):
                t2 = t2hi * SUBLANE + t2lo
                rows = _a_rows(t2hi, t2lo, hp)
                rhs = jnp.concatenate([are_ref[rows, :], aim_ref[rows, :]], axis=0).astype(BF16)
                y = jnp.dot(einv_ref[t2], rhs, preferred_element_type=F32)
                gate = _short_conv_column(g_ref, gw_ref, gb_ref, t2hi, t2lo, n2, half)
                z = (zc_ref if conv_z else z_ref)[:, _t2_rows(t2), :].reshape(half, LANE)
                o_ref[:, _t2_rows(t2), :] = (gate * (y + bias_ref[...] * z)).reshape(tiles)
            return carry
        lax.fori_loop(0, n2 // SUBLANE, body, 0, unroll=2)


def _conv_call(z, z_col0, g, g_col0, p, kr, ki, order, tb, conv_z):
    b, _, t1_tiles, tt, _ = z.shape
    n2, half = tt // SUBLANE, t1_tiles * SUBLANE
    n1, _, hp = _fft_dims(n2 * half)
    k1b = _k1_block(hp)
    per = D_HYENA // LANE
    seq = lambda col0: pl.BlockSpec((None, None, t1_tiles, tt, LANE), lambda c, i, s: (i, col0 + c, 0, 0, 0))
    taps = lambda col0: pl.BlockSpec((3, LANE), lambda c, i, s: (0, col0 + c))
    shift = lambda col0: pl.BlockSpec((1, LANE), lambda c, i, s: (0, col0 + c))
    spec_k = pl.BlockSpec((k1b * n2, LANE), lambda c, i, s: (s, order * per + c))
    a_shape = (n2 * hp, LANE)
    w_col0 = z_col0 if conv_z else g_col0
    return pl.pallas_call(
        functools.partial(_conv_kernel, n1=n1, n2=n2, hp=hp, k1b=k1b, conv_z=conv_z),
        out_shape=jax.ShapeDtypeStruct((b, per, t1_tiles, tt, LANE), F32),
        grid=(per, b, hp // k1b),
        in_specs=[seq(z_col0), seq(g_col0),
                  pl.BlockSpec((1, LANE), lambda c, i, s: (0, order * per + c)),
                  taps(w_col0), shift(w_col0), taps(g_col0), shift(g_col0),
                  _resident((n2, 2 * hp, n1)), _resident((n2, n1 // 2, 2 * hp)),
                  _resident((2 * n2, 2 * n2)), _resident((2 * n2, 2 * n2)),
                  spec_k, spec_k],
        out_specs=pl.BlockSpec((None, None, t1_tiles, tt, LANE), lambda c, i, s: (i, c, 0, 0, 0)),
        scratch_shapes=[pltpu.VMEM(a_shape, F32), pltpu.VMEM(a_shape, F32),
                        pltpu.VMEM((t1_tiles, tt, LANE) if conv_z else (SUBLANE, LANE), F32)],
        compiler_params=_params(("parallel", "parallel", "arbitrary")),
        name=f"hyena_conv_order{order}",
    )(z, g, p["hy_bias"], p["short_w"], p["short_b"], p["short_w"], p["short_b"],
      tb["e1"], tb["einv"], tb["f2c"], tb["f2ic"], kr, ki)


def _prepare(ffn1_w_gate, ffn1_w_up, ffn1_w_down, ln1_g, ln1_b, w_in,
             ssm_lam_re, ssm_lam_im, ssm_log_step, ssm_b_re, ssm_b_im, ssm_c_re, ssm_c_im,
             ssm_d, ssm_glu_w, ssm_glu_b, ssm_norm_g,
             hy_short_w, hy_short_b, hy_filt_w1, hy_filt_b1, hy_filt_w2, hy_filt_b2, hy_filt_w3,
             hy_sin_freq, hy_log_decay, hy_bias, hy_norm_g, w_out, ln2_g, ln2_b,
             ffn2_w_gate, ffn2_w_up, ffn2_w_down, ln3_g, ln3_b):
    row = lambda a: a[0].reshape(1, -1).astype(F32)
    hid = HYENA_FILTER_HIDDEN
    pad2 = lambda a, r, c: jnp.zeros((r, c), F32).at[:a.shape[0], :a.shape[1]].set(a.astype(F32))
    omega = jnp.exp(-math.log(HYENA_MAX_PERIOD) * jnp.arange(HYENA_BANDS, dtype=F32) / HYENA_BANDS)
    om = jnp.zeros((1, LANE), F32).at[0, 1:1 + HYENA_BANDS].set(omega).at[0, 1 + HYENA_BANDS:HYENA_POS_DIM].set(omega)
    p = dict(
        ffn1_wg=ffn1_w_gate[0].astype(BF16), ffn1_wu=ffn1_w_up[0].astype(BF16), ffn1_wd=ffn1_w_down[0].astype(BF16),
        ln1_g=row(ln1_g), ln1_b=row(ln1_b), w_in=w_in[0].astype(BF16),
        glu_w=ssm_glu_w[0].astype(BF16), glu_b=row(ssm_glu_b), ssm_norm_g=row(ssm_norm_g),
        hy_norm_g=row(hy_norm_g), w_out=w_out[0].astype(BF16), ln2_g=row(ln2_g), ln2_b=row(ln2_b),
        ffn2_wg=ffn2_w_gate[0].astype(BF16), ffn2_wu=ffn2_w_up[0].astype(BF16), ffn2_wd=ffn2_w_down[0].astype(BF16),
        ln3_g=row(ln3_g), ln3_b=row(ln3_b),
        short_w=hy_short_w[0].astype(F32), short_b=row(hy_short_b),
        filt_w1=pad2(hy_filt_w1[0], LANE, LANE), filt_b1=pad2(hy_filt_b1[0].reshape(1, hid), 1, LANE),
        filt_w2=pad2(hy_filt_w2[0], LANE, LANE), filt_b2=pad2(hy_filt_b2[0].reshape(1, hid), 1, LANE),
        filt_sf=pad2(hy_sin_freq[0], 2, LANE), filt_om=om,
        filt_w3=pad2(hy_filt_w3[0], LANE, hy_filt_w3.shape[-1]).astype(BF16), filt_ld=row(hy_log_decay),
        hy_bias=hy_bias[0].reshape(1, HYENA_ORDER * D_HYENA).astype(F32),
    )
    f32 = lambda a: a[0].astype(F32)
    p.update(_s5_tables(f32(ssm_lam_re), f32(ssm_lam_im), f32(ssm_log_step), f32(ssm_b_re), f32(ssm_b_im),
                        f32(ssm_c_re), f32(ssm_c_im), f32(ssm_d)))
    return p


def _trunk(x, p):
    b, l, d = x.shape
    per = D_HYENA // LANE
    h1, u, hy_raw = _ffn1_call(x, p)
    y_ssm = _s5_call(u.reshape(b, l, D_SSM), p)
    tb = _dft_tables(l)
    kr, ki = _filter_call(l, p, tb)
    z1 = _conv_call(hy_raw, 0, hy_raw, per, p, kr, ki, 0, tb, True)
    z2 = _conv_call(z1, 0, hy_raw, 2 * per, p, kr, ki, 1, tb, False)
    return _final_call(y_ssm, z2, h1.reshape(b, l, d), p)


def kernel(x_prompt, x_sample, ffn1_w_gate, ffn1_w_up, ffn1_w_down, ln1_g, ln1_b, w_in, ssm_lam_re, ssm_lam_im, ssm_log_step, ssm_b_re, ssm_b_im, ssm_c_re, ssm_c_im, ssm_d, ssm_glu_w, ssm_glu_b, ssm_norm_g, hy_short_w, hy_short_b, hy_filt_w1, hy_filt_b1, hy_filt_w2, hy_filt_b2, hy_filt_w3, hy_sin_freq, hy_log_decay, hy_bias, hy_norm_g, w_out, ln2_g, ln2_b, ffn2_w_gate, ffn2_w_up, ffn2_w_down, ln3_g, ln3_b):
    p = _prepare(ffn1_w_gate, ffn1_w_up, ffn1_w_down, ln1_g, ln1_b, w_in,
                 ssm_lam_re, ssm_lam_im, ssm_log_step, ssm_b_re, ssm_b_im, ssm_c_re, ssm_c_im,
                 ssm_d, ssm_glu_w, ssm_glu_b, ssm_norm_g,
                 hy_short_w, hy_short_b, hy_filt_w1, hy_filt_b1, hy_filt_w2, hy_filt_b2, hy_filt_w3,
                 hy_sin_freq, hy_log_decay, hy_bias, hy_norm_g, w_out, ln2_g, ln2_b,
                 ffn2_w_gate, ffn2_w_up, ffn2_w_down, ln3_g, ln3_b)
    return (_trunk(x_prompt, p), _trunk(x_sample, p))
```
